```python
import math
import jax, jax.numpy as jnp
from jax import lax
import numpy as np

D_MODEL = 1024
BATCH = 8
SEQ = 4096
DEPTH = 4
DEC_BATCH = 8
DEC_SEQ = 64
PAST_LEN = 2048

CHUNK = 64
Q_BLOCK = 128
RMS_EPS = 1e-6
N_MIXERS = 3
N_POOL_LAYERS = (DEPTH + 2) // 3
N_DN_LAYERS = (DEPTH + 1) // 3
N_SB_LAYERS = DEPTH // 3
POOL_WINDOWS = (2, 4, 8, 16)
POOL_GROUPS = len(POOL_WINDOWS)
POOL_GC = D_MODEL // POOL_GROUPS
POOL_BUF = max(POOL_WINDOWS) - 1
DN_HEADS = 8
DN_DK = D_MODEL // DN_HEADS
DN_DV = D_MODEL // DN_HEADS
DN_KW = DN_HEADS * DN_DK
DN_VW = DN_HEADS * DN_DV
DN_QKV = 2 * DN_KW + DN_VW
DN_CONV = 4
SB_HEADS = 16
SB_DH = D_MODEL // SB_HEADS
D_FF = ((8 * D_MODEL // 3 + 127) // 128) * 128
FFN_CONV = 3

kernel_name = "hybrid_pool_gdn_stickbreak_stream_step"


def _rmsnorm(x, g):
    xf = x.astype(jnp.float32)
    y = xf * lax.rsqrt(jnp.mean(xf * xf, axis=-1, keepdims=True) + RMS_EPS)
    return (y * g.astype(jnp.float32)).astype(x.dtype)


def _l2norm(x):
    xf = x.astype(jnp.float32)
    return xf * lax.rsqrt(jnp.sum(xf * xf, axis=-1, keepdims=True) + RMS_EPS)


def _causal_dwconv(x, buf, w):
    width = w.shape[0]
    L = x.shape[1]
    xp = jnp.concatenate([buf.astype(x.dtype), x], axis=1)
    y = xp[:, 0:L] * w[0]
    for i in range(1, width):
        y = y + xp[:, i:i + L] * w[i]
    return y, xp[:, xp.shape[1] - (width - 1):]


def _pool_mixer(h, buf, w, scale, start):
    B, L, D = h.shape
    ext = jnp.concatenate([buf.astype(h.dtype), h], axis=1).astype(jnp.float32)
    cs = jnp.concatenate([jnp.zeros((B, 1, D), jnp.float32), jnp.cumsum(ext, axis=1)], axis=1)
    n_avail = (start + jnp.arange(L) + 1).astype(jnp.float32)
    hi = cs[:, POOL_BUF + 1:]
    means = []
    for g, win in enumerate(POOL_WINDOWS):
        sl = slice(g * POOL_GC, (g + 1) * POOL_GC)
        lo = cs[:, POOL_BUF + 1 - win:POOL_BUF + 1 - win + L, sl]
        cnt = jnp.minimum(n_avail, float(win))
        means.append((hi[..., sl] - lo) / cnt[None, :, None])
    d = jnp.concatenate(means, axis=-1) - ext[:, POOL_BUF:]
    y = jnp.einsum('blgc,gce->blge', d.reshape(B, L, POOL_GROUPS, POOL_GC), w.astype(jnp.float32))
    y = y.reshape(B, L, D) * scale.astype(jnp.float32)
    return y.astype(h.dtype), ext[:, ext.shape[1] - POOL_BUF:].astype(buf.dtype)


def _gated_delta_chunked(q, k, v, beta, g, s0, chunk):
    B, L, H, DK = q.shape
    DV = v.shape[-1]
    n = L // chunk

    def blk(t):
        t = t.reshape((B, n, chunk, H) + t.shape[3:])
        return jnp.moveaxis(jnp.swapaxes(t, 2, 3), 1, 0)

    q, k, v, beta, g = blk(q), blk(k), blk(v), blk(beta), blk(g)
    gc = jnp.cumsum(g, axis=-1)
    idx = jnp.arange(chunk)
    lower_incl = idx[:, None] >= idx[None, :]
    lower_strict = idx[:, None] > idx[None, :]
    diff = gc[..., :, None] - gc[..., None, :]
    decay_incl = jnp.exp(jnp.where(lower_incl, diff, -jnp.inf))
    decay_strict = jnp.where(lower_strict, decay_incl, 0.0)
    kb = k * beta[..., None]
    m = jnp.einsum('nbhid,nbhjd->nbhij', kb, k) * decay_strict
    eye = jnp.eye(chunk, dtype=jnp.float32)
    rhs = jnp.concatenate([v * beta[..., None], kb * jnp.exp(gc)[..., None]], axis=-1)
    sol = lax.linalg.triangular_solve(eye + m, rhs, left_side=True, lower=True, unit_diagonal=True)
    u, w = sol[..., :DV], sol[..., DV:]
    attn = jnp.einsum('nbhid,nbhjd->nbhij', q, k) * decay_incl
    qg = q * jnp.exp(gc)[..., None]
    kg = k * jnp.exp(gc[..., -1:] - gc)[..., None]
    glast = jnp.exp(gc[..., -1])

    def step(s, xs):
        u_c, w_c, qg_c, kg_c, attn_c, gl_c = xs
        v_new = u_c - jnp.einsum('bhck,bhkv->bhcv', w_c, s)
        o_c = jnp.einsum('bhck,bhkv->bhcv', qg_c, s) + jnp.einsum('bhij,bhjv->bhiv', attn_c, v_new)
        s = s * gl_c[..., None, None] + jnp.einsum('bhck,bhcv->bhkv', kg_c, v_new)
        return s, o_c

    s_fin, o = lax.scan(step, s0, (u, w, qg, kg, attn, glast))
    o = jnp.swapaxes(jnp.moveaxis(o, 0, 1), 2, 3).reshape(B, L, H, DV)
    return o, s_fin


def _gated_delta_mixer(h, conv_buf, s0, w_in, conv_w, a_log, dt_bias, norm_w, w_out, chunk):
    B, L, _ = h.shape
    proj = h @ w_in
    qkv, z, a, b = jnp.split(proj, [DN_QKV, DN_QKV + DN_VW, DN_QKV + DN_VW + DN_HEADS], axis=-1)
    qkv, new_conv = _causal_dwconv(qkv, conv_buf, conv_w)
    qkv = jax.nn.silu(qkv)
    q, k, v = jnp.split(qkv, [DN_KW, 2 * DN_KW], axis=-1)
    q = _l2norm(q.reshape(B, L, DN_HEADS, DN_DK)) * (DN_DK ** -0.5)
    k = _l2norm(k.reshape(B, L, DN_HEADS, DN_DK))
    v = v.reshape(B, L, DN_HEADS, DN_DV).astype(jnp.float32)
    beta = jax.nn.sigmoid(b.astype(jnp.float32))
    g = -jnp.exp(a_log.astype(jnp.float32)) * jax.nn.softplus(a.astype(jnp.float32) + dt_bias.astype(jnp.float32))
    o, s_new = _gated_delta_chunked(q, k, v, beta, g, s0.astype(jnp.float32), chunk)
    o = _rmsnorm(o, norm_w) * jax.nn.silu(z.reshape(B, L, DN_HEADS, DN_DV).astype(jnp.float32))
    y = o.reshape(B, L, DN_VW).astype(h.dtype) @ w_out
    return y, new_conv, s_new.astype(s0.dtype)


def _sb_attend(q, k, v, q_pos, k_pos):
    z = jnp.einsum('bqhd,bshd->bhqs', q, k).astype(jnp.float32) * (SB_DH ** -0.5)
    mask = k_pos[None, :] < q_pos[:, None]
    log_1m = jnp.where(mask, jax.nn.log_sigmoid(-z), 0.0)
    rest = lax.cumsum(log_1m, axis=3, reverse=True) - log_1m
    a = jnp.where(mask, jnp.exp(jax.nn.log_sigmoid(z) + rest), 0.0)
    return jnp.einsum('bhqs,bshd->bqhd', a, v.astype(jnp.float32))


def _sb_mixer(h, k_past, v_past, w_qkv, w_out, start):
    B, L, _ = h.shape
    q, k, v = jnp.split(h @ w_qkv, 3, axis=-1)
    q = q.reshape(B, L, SB_HEADS, SB_DH)
    k = k.reshape(B, L, SB_HEADS, SB_DH)
    v = v.reshape(B, L, SB_HEADS, SB_DH)
    if k_past is None:
        nq = L // Q_BLOCK
        qb = jnp.moveaxis(q.reshape(B, nq, Q_BLOCK, SB_HEADS, SB_DH), 1, 0)
        pb = jnp.arange(L).reshape(nq, Q_BLOCK)
        k_pos = jnp.arange(L)
        o = lax.map(lambda qp: _sb_attend(qp[0], k, v, qp[1], k_pos), (qb, pb))
        o = jnp.moveaxis(o, 0, 1).reshape(B, L, SB_HEADS, SB_DH)
    else:
        k_all = jnp.concatenate([k_past.astype(k.dtype), k], axis=1)
        v_all = jnp.concatenate([v_past.astype(v.dtype), v], axis=1)
        o = _sb_attend(q, k_all, v_all, start + jnp.arange(L), jnp.arange(k_all.shape[1]))
    y = o.reshape(B, L, SB_HEADS * SB_DH).astype(h.dtype) @ w_out
    return y, k, v


def _conv_ffn(h, buf, w_up, conv_w, conv_b, w_down):
    val, gate = jnp.split(h @ w_up, 2, axis=-1)
    gate, new_buf = _causal_dwconv(gate, buf, conv_w)
    return (jax.nn.silu(gate + conv_b) * val) @ w_down, new_buf


def _trunk(x, pool_bufs, dn_conv_bufs, dn_states, sb_k_past, sb_v_past, ffn_bufs, start, dn_chunk, p):
    n_pool, n_dnc, n_dn, n_k, n_v, n_ffn = [], [], [], [], [], []
    for i in range(DEPTH):
        kind, j = i % N_MIXERS, i // N_MIXERS
        h = _rmsnorm(x, p['mix_norm'][i])
        if kind == 0:
            y, buf = _pool_mixer(h, pool_bufs[j], p['pool_w'][j], p['pool_scale'][j], start)
            n_pool.append(buf)
        elif kind == 1:
            y, cbuf, s = _gated_delta_mixer(h, dn_conv_bufs[j], dn_states[j], p['dn_w_in'][j], p['dn_conv_w'][j],
                                            p['dn_a_log'][j], p['dn_dt_bias'][j], p['dn_norm'][j],
                                            p['dn_w_out'][j], dn_chunk)
            n_dnc.append(cbuf)
            n_dn.append(s)
        else:
            kp = None if sb_k_past is None else sb_k_past[j]
            vp = None if sb_v_past is None else sb_v_past[j]
            y, kn, vn = _sb_mixer(h, kp, vp, p['sb_w_qkv'][j], p['sb_w_out'][j], start)
            n_k.append(kn)
            n_v.append(vn)
        x = x + y
        h = _rmsnorm(x, p['ffn_norm'][i])
        y, fbuf = _conv_ffn(h, ffn_bufs[i], p['ffn_w_up'][i], p['ffn_conv_w'][i], p['ffn_conv_b'][i],
                            p['ffn_w_down'][i])
        n_ffn.append(fbuf)
        x = x + y
    y = _rmsnorm(x, p['final_norm'])
    return (y, jnp.stack(n_pool), jnp.stack(n_dnc), jnp.stack(n_dn), jnp.stack(n_k), jnp.stack(n_v),
            jnp.stack(n_ffn))


def setup_inputs(seed: int = 0) -> dict:
    key = jax.random.key(seed)
    ks = list(jax.random.split(key, 32))
    f32 = jnp.float32

    def nrm(k, shape, fan_in):
        return jax.random.normal(k, shape, f32) * (fan_in ** -0.5)

    def gain(k, shape):
        return 1.0 + 0.02 * jax.random.normal(k, shape, f32)

    dt = jnp.exp(jax.random.uniform(ks[20], (N_DN_LAYERS, DN_HEADS), f32, math.log(1e-3), math.log(1e-1)))
    return {
        "x_prompt": jax.random.normal(ks[0], (BATCH, SEQ, D_MODEL), f32),
        "x_sample": jax.random.normal(ks[1], (DEC_BATCH, DEC_SEQ, D_MODEL), f32),
        "state_pool": jax.random.normal(ks[2], (N_POOL_LAYERS, DEC_BATCH, POOL_BUF, D_MODEL), f32),
        "state_dn_conv": jax.random.normal(ks[3], (N_DN_LAYERS, DEC_BATCH, DN_CONV - 1, DN_QKV), f32),
        "state_dn": 0.1 * jax.random.normal(ks[4], (N_DN_LAYERS, DEC_BATCH, DN_HEADS, DN_DK, DN_DV), f32),
        "cache_sb_k": jax.random.normal(ks[5], (N_SB_LAYERS, DEC_BATCH, PAST_LEN, SB_HEADS, SB_DH), f32),
        "cache_sb_v": jax.random.normal(ks[6], (N_SB_LAYERS, DEC_BATCH, PAST_LEN, SB_HEADS, SB_DH), f32),
        "state_ffn_conv": jax.random.normal(ks[7], (DEPTH, DEC_BATCH, FFN_CONV - 1, D_FF), f32),
        "mix_norm": gain(ks[8], (DEPTH, D_MODEL)),
        "ffn_norm": gain(ks[9], (DEPTH, D_MODEL)),
        "final_norm": gain(ks[10], (D_MODEL,)),
        "pool_w": nrm(ks[11], (N_POOL_LAYERS, POOL_GROUPS, POOL_GC, POOL_GC), POOL_GC),
        "pool_scale": gain(ks[12], (N_POOL_LAYERS, D_MODEL)),
        "dn_w_in": nrm(ks[13], (N_DN_LAYERS, D_MODEL, DN_QKV + DN_VW + 2 * DN_HEADS), D_MODEL),
        "dn_conv_w": nrm(ks[14], (N_DN_LAYERS, DN_CONV, DN_QKV), DN_CONV),
        "dn_a_log": jnp.log(jax.random.uniform(ks[15], (N_DN_LAYERS, DN_HEADS), f32, 1.0, 16.0)),
        "dn_dt_bias": jnp.log(jnp.expm1(dt)),
        "dn_norm": gain(ks[16], (N_DN_LAYERS, DN_DV)),
        "dn_w_out": nrm(ks[17], (N_DN_LAYERS, DN_VW, D_MODEL), DN_VW),
        "sb_w_qkv": nrm(ks[18], (N_SB_LAYERS, D_MODEL, 3 * SB_HEADS * SB_DH), D_MODEL),
        "sb_w_out": nrm(ks[19], (N_SB_LAYERS, SB_HEADS * SB_DH, D_MODEL), SB_HEADS * SB_DH),
        "ffn_w_up": nrm(ks[21], (DEPTH, D_MODEL, 2 * D_FF), D_MODEL),
        "ffn_conv_w": nrm(ks[22], (DEPTH, FFN_CONV, D_FF), FFN_CONV),
        "ffn_conv_b": 0.01 * jax.random.normal(ks[23], (DEPTH, D_FF), f32),
        "ffn_w_down": nrm(ks[24], (DEPTH, D_FF, D_MODEL), D_FF),
    }


def reference(x_prompt, x_sample, state_pool, state_dn_conv, state_dn, cache_sb_k, cache_sb_v, state_ffn_conv,
              mix_norm, ffn_norm, final_norm, pool_w, pool_scale, dn_w_in, dn_conv_w, dn_a_log, dn_dt_bias,
              dn_norm, dn_w_out, sb_w_qkv, sb_w_out, ffn_w_up, ffn_conv_w, ffn_conv_b, ffn_w_down):
    p = dict(mix_norm=mix_norm, ffn_norm=ffn_norm, final_norm=final_norm, pool_w=pool_w, pool_scale=pool_scale,
             dn_w_in=dn_w_in, dn_conv_w=dn_conv_w, dn_a_log=dn_a_log, dn_dt_bias=dn_dt_bias, dn_norm=dn_norm,
             dn_w_out=dn_w_out, sb_w_qkv=sb_w_qkv, sb_w_out=sb_w_out, ffn_w_up=ffn_w_up, ffn_conv_w=ffn_conv_w,
             ffn_conv_b=ffn_conv_b, ffn_w_down=ffn_w_down)
    bp = x_prompt.shape[0]
    dtp = x_prompt.dtype
    zero_pool = jnp.zeros((N_POOL_LAYERS, bp, POOL_BUF, D_MODEL), dtp)
    zero_dnc = jnp.zeros((N_DN_LAYERS, bp, DN_CONV - 1, DN_QKV), dtp)
    zero_dn = jnp.zeros((N_DN_LAYERS, bp, DN_HEADS, DN_DK, DN_DV), state_dn.dtype)
    zero_ffn = jnp.zeros((DEPTH, bp, FFN_CONV - 1, D_FF), dtp)
    y_prompt, pool_p, dnc_p, dn_p, k_p, v_p, ffn_p = _trunk(
        x_prompt, zero_pool, zero_dnc, zero_dn, None, None, zero_ffn, 0, CHUNK, p)
    y_sample, pool_s, dnc_s, dn_s, k_s, v_s, ffn_s = _trunk(
        x_sample, state_pool, state_dn_conv, state_dn, cache_sb_k, cache_sb_v, state_ffn_conv,
        PAST_LEN, x_sample.shape[1], p)
    return (y_prompt, y_sample, pool_p, pool_s, dnc_p, dnc_s, dn_p, dn_s, k_p, k_s, v_p, v_s, ffn_p, ffn_s)
```

```python
import functools

import jax
import jax.numpy as jnp
from jax import lax
from jax.experimental import pallas as pl
from jax.experimental.pallas import tpu as pltpu

F32 = jnp.float32
BF16 = jnp.bfloat16

RMS_EPS = 1e-6
POOL_WINDOWS = (2, 4, 8, 16)
POOL_BUF = max(POOL_WINDOWS) - 1
DN_HEADS = 8
DN_DK = 128
DN_CONV = 4
DN_CHUNK = 64
SB_HEADS = 16
SB_DH = 64
FFN_CONV = 3

LANES = 128
MXU_N = 256
VMEM_LIMIT = 56 * 1024 * 1024
ROW_TILE = 512


def _params(n_axes, vmem=VMEM_LIMIT):
    return pltpu.CompilerParams(dimension_semantics=("arbitrary",) * n_axes, vmem_limit_bytes=vmem)


def _const_spec(shape):
    nd = len(shape)
    return pl.BlockSpec(shape, lambda *_: (0,) * nd, pipeline_mode=pl.Buffered(1))


def _rms_rows(x, g):
    ms = jnp.mean(x * x, axis=-1, keepdims=True)
    return x * lax.rsqrt(ms + RMS_EPS) * g


def _silu(x):
    return x * jax.nn.sigmoid(x)


def _softplus(x):
    return jnp.maximum(x, 0.0) + jnp.log1p(jnp.exp(-jnp.abs(x)))


def _dot(a, b):
    return jnp.dot(a, b, preferred_element_type=F32)


def _dot_nt(a, b, precision=None):
    return lax.dot_general(a, b, (((1,), (1,)), ((), ())), preferred_element_type=F32, precision=precision)


def _tiling(B, L):
    tl = min(L, ROW_TILE)
    nb = max(1, min(B, ROW_TILE // tl))
    assert L % tl == 0 and B % nb == 0
    return nb, tl


def _ffn_body(*refs, nb, tl, n_chunks, final):
    (x_ref, buf_ref, ng_ref, wv_ref, wg_ref, cw_ref, cb_ref, wd_ref) = refs[:8]
    k = 8
    if final:
        fg_ref = refs[k]
        k += 1
    o_ref, nbuf_ref = refs[k], refs[k + 1]
    k += 2
    if final:
        y_ref = refs[k]
        k += 1
    h_scr, acc_scr, act_scr, cv_scr, carry_scr = refs[k:]
    t = pl.program_id(1)
    tm = nb * tl
    d = x_ref.shape[-1]
    pad = 8

    x = x_ref[...].reshape(tm, d)
    h_scr[...] = _rms_rows(x, ng_ref[...]).astype(BF16)
    acc_scr[...] = jnp.zeros_like(acc_scr)

    @pl.when(t == 0)
    def _():
        carry_scr[...] = buf_ref[...]

    def chunk(j, c):
        h = h_scr[...]
        hv = _dot(h, wv_ref[j])
        hg = _dot(h, wg_ref[j])
        cw = cw_ref[j]
        cb = cb_ref[j]
        for b in range(nb):
            gb = hg[b * tl:(b + 1) * tl]
            cv_scr[b, pad - 2:pad, :] = carry_scr[b, j]
            cv_scr[b, pad:pad + tl, :] = gb
            conv = (cv_scr[b, pad - 2:pad - 2 + tl, :] * cw[0:1] + cv_scr[b, pad - 1:pad - 1 + tl, :] * cw[1:2]
                    + gb * cw[2:3])
            last = gb[tl - 2:tl]
            carry_scr[b, j] = last
            nbuf_ref[b, j] = last
            act = _silu(conv + cb) * hv[b * tl:(b + 1) * tl]
            act_scr[b * tl:(b + 1) * tl, :] = act.astype(BF16)
        acc_scr[...] += _dot(act_scr[...], wd_ref[j])
        return c

    lax.fori_loop(0, n_chunks, chunk, 0)
    out = x_ref[...].reshape(tm, d) + acc_scr[...]
    o_ref[...] = out.reshape(nb, tl, d)
    if final:
        y_ref[...] = _rms_rows(out, fg_ref[...]).reshape(nb, tl, d)


def _conv_ffn(x, buf, norm_g, w_up, conv_w, conv_b, w_down, final_g=None):
    B, L, D = x.shape
    F = w_down.shape[0]
    tf = MXU_N
    n_chunks = F // tf
    assert F % tf == 0
    nb, tl = _tiling(B, L)
    tm = nb * tl
    final = final_g is not None

    wv = w_up[:, :F].astype(BF16).reshape(D, n_chunks, tf).transpose(1, 0, 2)
    wg = w_up[:, F:].astype(BF16).reshape(D, n_chunks, tf).transpose(1, 0, 2)
    wd = w_down.astype(BF16).reshape(n_chunks, tf, D)
    cw = conv_w.reshape(FFN_CONV, n_chunks, tf).transpose(1, 0, 2)
    cb = conv_b.reshape(n_chunks, 1, tf)
    bufc = buf.astype(F32).reshape(B, FFN_CONV - 1, n_chunks, tf).transpose(0, 2, 1, 3)

    row_spec = pl.BlockSpec((nb, tl, D), lambda i, t: (i, t, 0))
    buf_spec = pl.BlockSpec((nb, n_chunks, FFN_CONV - 1, tf), lambda i, t: (i, 0, 0, 0))
    in_specs = [row_spec, buf_spec, _const_spec((1, D)), _const_spec(wv.shape), _const_spec(wg.shape),
                _const_spec(cw.shape), _const_spec(cb.shape), _const_spec(wd.shape)]
    args = [x, bufc, norm_g.reshape(1, D), wv, wg, cw, cb, wd]
    out_shape = [jax.ShapeDtypeStruct((B, L, D), F32),
                 jax.ShapeDtypeStruct((B, n_chunks, FFN_CONV - 1, tf), F32)]
    out_specs = [row_spec, buf_spec]
    if final:
        in_specs.append(_const_spec((1, D)))
        args.append(final_g.reshape(1, D))
        out_shape.append(jax.ShapeDtypeStruct((B, L, D), F32))
        out_specs.append(row_spec)
    scratch = [pltpu.VMEM((tm, D), BF16), pltpu.VMEM((tm, D), F32), pltpu.VMEM((tm, tf), BF16),
               pltpu.VMEM((nb, tl + 8, tf), F32), pltpu.VMEM((nb, n_chunks, FFN_CONV - 1, tf), F32)]
    outs = pl.pallas_call(
        functools.partial(_ffn_body, nb=nb, tl=tl, n_chunks=n_chunks, final=final),
        grid=(B // nb, L // tl), in_specs=in_specs, out_specs=out_specs, out_shape=out_shape,
        scratch_shapes=scratch, compiler_params=_params(2), name="conv_ffn")(*args)
    nbuf = outs[1].transpose(0, 2, 1, 3).reshape(B, FFN_CONV - 1, F)
    return outs[0], nbuf, (outs[2] if final else None)


def _pool_body(x_ref, buf_ref, ng_ref, w_ref, sc_ref, o_ref, nbuf_ref, ext_scr, *, nb, tl, start):
    t = pl.program_id(1)
    d = x_ref.shape[-1]
    hist = POOL_BUF + 1
    gc = d // len(POOL_WINDOWS)
    row = lax.broadcasted_iota(jnp.int32, (tl, 1), 0)
    n_avail = start + t * tl + row + 1
    for b in range(nb):
        @pl.when(t == 0)
        def _():
            ext_scr[b, 0:hist, :] = jnp.zeros((hist, d), F32)
            ext_scr[b, 1:hist, :] = buf_ref[b]

        @pl.when(t > 0)
        def _():
            ext_scr[b, 0:hist, :] = ext_scr[b, tl:tl + hist, :]

        x = x_ref[b]
        h = _rms_rows(x, ng_ref[...])
        ext_scr[b, hist:hist + tl, :] = h
        for g, win in enumerate(POOL_WINDOWS):
            cols = slice(g * gc, (g + 1) * gc)
            s = h[:, cols]
            for i in range(1, win):
                s = s + ext_scr[b, hist - i:hist - i + tl, cols]
            cnt = jnp.minimum(n_avail, win).astype(F32)
            dmean = s * (1.0 / cnt) - h[:, cols]
            y = _dot(dmean.astype(BF16), w_ref[g])
            o_ref[b, :, cols] = x[:, cols] + y * sc_ref[:, cols]
        nbuf_ref[b] = ext_scr[b, tl + 1:tl + hist, :]


def _pool_mixer(x, buf, norm_g, w, scale, start):
    B, L, D = x.shape
    nb, tl = _tiling(B, L)
    row_spec = pl.BlockSpec((nb, tl, D), lambda i, t: (i, t, 0))
    buf_spec = pl.BlockSpec((nb, POOL_BUF, D), lambda i, t: (i, 0, 0))
    wb = w.astype(BF16)
    return pl.pallas_call(
        functools.partial(_pool_body, nb=nb, tl=tl, start=start),
        grid=(B // nb, L // tl),
        in_specs=[row_spec, buf_spec, _const_spec((1, D)), _const_spec(wb.shape), _const_spec((1, D))],
        out_specs=[row_spec, buf_spec],
        out_shape=[jax.ShapeDtypeStruct((B, L, D), F32), jax.ShapeDtypeStruct((B, POOL_BUF, D), F32)],
        scratch_shapes=[pltpu.VMEM((nb, tl + POOL_BUF + 1, D), F32)],
        compiler_params=_params(2), name="pool_mixer")(x, buf.astype(F32), norm_g.reshape(1, D), wb,
                                                      scale.reshape(1, D))


def _gdn_in_body(x_ref, cbuf_ref, ng_ref, wqkv_ref, cw_ref, wz_ref, wab_ref, alog_ref, dtb_ref,
                 q_ref, k_ref, v_ref, z_ref, gb_ref, ncbuf_ref, h_scr, cv_scr, carry_scr, *, nb, tl):
    t = pl.program_id(1)
    tm = nb * tl
    d = x_ref.shape[-1]
    pad = 8
    tc = MXU_N
    n_sec = d // tc
    hw = DN_CONV - 1

    x = x_ref[...].reshape(tm, d)
    h_scr[...] = _rms_rows(x, ng_ref[...]).astype(BF16)
    h = h_scr[...]

    @pl.when(t == 0)
    def _():
        carry_scr[...] = cbuf_ref[...]

    outs = (q_ref, k_ref, v_ref)
    for j in range(3 * n_sec):
        sec, jc = divmod(j, n_sec)
        y = _dot(h, wqkv_ref[j])
        cw = cw_ref[j]
        for b in range(nb):
            yb = y[b * tl:(b + 1) * tl]
            cv_scr[b, pad - hw:pad, :] = carry_scr[b, j]
            cv_scr[b, pad:pad + tl, :] = yb
            conv = yb * cw[hw:hw + 1]
            for i in range(hw):
                conv = conv + cv_scr[b, pad - hw + i:pad - hw + i + tl, :] * cw[i:i + 1]
            last = yb[tl - hw:tl]
            carry_scr[b, j] = last
            ncbuf_ref[b, j] = last
            s = _silu(conv)
            for half in range(tc // DN_DK):
                sh = s[:, half * DN_DK:(half + 1) * DN_DK]
                if sec < 2:
                    sh = sh * lax.rsqrt(jnp.sum(sh * sh, axis=-1, keepdims=True) + RMS_EPS)
                if sec == 0:
                    sh = sh * (DN_DK ** -0.5)
                c0 = jc * tc + half * DN_DK
                outs[sec][b, :, c0:c0 + DN_DK] = sh
    for j in range(n_sec):
        z_ref[:, :, j * tc:(j + 1) * tc] = _dot(h, wz_ref[j]).reshape(nb, tl, tc)
    ab = _dot(h, wab_ref[...])
    g = -jnp.exp(alog_ref[...]) * _softplus(ab + dtb_ref[...])
    lane = lax.broadcasted_iota(jnp.int32, ab.shape, 1)
    gb_ref[...] = jnp.where(lane < DN_HEADS, g, jax.nn.sigmoid(ab)).reshape(nb, tl, LANES)


def _gdn_in(x, cbuf, norm_g, w_in, conv_w, a_log, dt_bias):
    B, L, D = x.shape
    nb, tl = _tiling(B, L)
    tm = nb * tl
    tc = MXU_N
    qkv_w = 3 * D
    n_qkv = qkv_w // tc
    n_sec = D // tc
    wqkv = w_in[:, :qkv_w].astype(BF16).reshape(D, n_qkv, tc).transpose(1, 0, 2)
    wz = w_in[:, qkv_w:qkv_w + D].astype(BF16).reshape(D, n_sec, tc).transpose(1, 0, 2)
    wab = jnp.zeros((D, LANES), BF16).at[:, :2 * DN_HEADS].set(w_in[:, qkv_w + D:].astype(BF16))
    cw = conv_w.reshape(DN_CONV, n_qkv, tc).transpose(1, 0, 2)
    alog = jnp.zeros((1, LANES), F32).at[0, :DN_HEADS].set(a_log)
    dtb = jnp.zeros((1, LANES), F32).at[0, :DN_HEADS].set(dt_bias)
    cbufc = cbuf.astype(F32).reshape(B, DN_CONV - 1, n_qkv, tc).transpose(0, 2, 1, 3)

    row_spec = pl.BlockSpec((nb, tl, D), lambda i, t: (i, t, 0))
    gb_spec = pl.BlockSpec((nb, tl, LANES), lambda i, t: (i, t, 0))
    cb_spec = pl.BlockSpec((nb, n_qkv, DN_CONV - 1, tc), lambda i, t: (i, 0, 0, 0))
    act = jax.ShapeDtypeStruct((B, L, D), F32)
    outs = pl.pallas_call(
        functools.partial(_gdn_in_body, nb=nb, tl=tl),
        grid=(B // nb, L // tl),
        in_specs=[row_spec, cb_spec, _const_spec((1, D)), _const_spec(wqkv.shape), _const_spec(cw.shape),
                  _const_spec(wz.shape), _const_spec(wab.shape), _const_spec((1, LANES)),
                  _const_spec((1, LANES))],
        out_specs=[row_spec, row_spec, row_spec, row_spec, gb_spec, cb_spec],
        out_shape=[act, act, act, act, jax.ShapeDtypeStruct((B, L, LANES), F32),
                   jax.ShapeDtypeStruct((B, n_qkv, DN_CONV - 1, tc), F32)],
        scratch_shapes=[pltpu.VMEM((tm, D), BF16), pltpu.VMEM((nb, tl + 8, tc), F32),
                        pltpu.VMEM((nb, n_qkv, DN_CONV - 1, tc), F32)],
        compiler_params=_params(2), name="gdn_in")(x, cbufc, norm_g.reshape(1, D), wqkv, cw, wz, wab, alog, dtb)
    q, k, v, z, gb, ncbuf = outs
    return q, k, v, z, gb, ncbuf.transpose(0, 2, 1, 3).reshape(B, DN_CONV - 1, qkv_w)


def _gdn_delta_body(q_ref, k_ref, v_ref, gb_ref, s0_ref, o_ref, sout_ref, s_scr, *, n_chunks):
    c_idx = pl.program_id(1)
    C = DN_CHUNK
    dk = DN_DK
    hp = lax.Precision.HIGHEST

    @pl.when(c_idx == 0)
    def _():
        s_scr[...] = s0_ref[0]

    ii = lax.broadcasted_iota(jnp.int32, (C, C), 0)
    jj = lax.broadcasted_iota(jnp.int32, (C, C), 1)
    lower_incl = ii >= jj
    lower_strict = ii > jj
    tri = lower_incl.astype(F32)
    eye = (ii == jj).astype(F32)
    zpad = jnp.zeros((LANES - C, LANES), F32)

    def chunk(ci, carry):
        r0 = pl.multiple_of(ci * C, C)
        rows = pl.ds(r0, C)
        gb = gb_ref[0, rows, :]
        gcs = jnp.dot(tri, gb, preferred_element_type=F32, precision=hp)
        gcs_t = jnp.concatenate([gcs, zpad], axis=0).T
        for h in range(DN_HEADS):
            cols = slice(h * dk, (h + 1) * dk)
            q = q_ref[0, rows, cols]
            k = k_ref[0, rows, cols]
            v = v_ref[0, rows, cols]
            gcol = gcs[:, h:h + 1]
            grow = gcs_t[h:h + 1, 0:C]
            bcol = gb[:, DN_HEADS + h:DN_HEADS + h + 1]
            diff = jnp.where(lower_incl, gcol - grow, 0.0)
            dec_incl = jnp.where(lower_incl, jnp.exp(diff), 0.0)
            dec_strict = jnp.where(lower_strict, dec_incl, 0.0)
            kb = k * bcol
            m = _dot_nt(kb, k, hp) * dec_strict
            p = -m
            tinv = eye + p
            for _ in range(5):
                p = jnp.dot(p, p, preferred_element_type=F32, precision=hp)
                tinv = tinv + jnp.dot(tinv, p, preferred_element_type=F32, precision=hp)
            egc = jnp.exp(gcol)
            rhs = jnp.concatenate([v * bcol, kb * egc], axis=1)
            uw = jnp.dot(tinv, rhs, preferred_element_type=F32, precision=hp)
            u, w = uw[:, :dk], uw[:, dk:]
            attn = _dot_nt(q, k, hp) * dec_incl
            qg = q * egc
            glast = grow[:, C - 1:C]
            k_t = jnp.concatenate([k, zpad], axis=0).T[:, 0:C]
            kg_t = k_t * jnp.exp(glast - grow)
            s = s_scr[h]
            ws_qs = jnp.dot(jnp.concatenate([w, qg], axis=0), s, preferred_element_type=F32, precision=hp)
            v_new = u - ws_qs[:C]
            o = ws_qs[C:] + jnp.dot(attn, v_new, preferred_element_type=F32, precision=hp)
            o_ref[0, rows, cols] = o
            s_scr[h] = s * jnp.exp(glast) + jnp.dot(kg_t, v_new, preferred_element_type=F32, precision=hp)
        return carry

    lax.fori_loop(0, n_chunks, chunk, 0)
    sout_ref[0] = s_scr[...]


def _gdn_delta(q, k, v, gb, s0):
    B, L, D = q.shape
    G = min(L, 1024)
    assert L % G == 0 and G % DN_CHUNK == 0
    row_spec = pl.BlockSpec((1, G, D), lambda b, c: (b, c, 0))
    gb_spec = pl.BlockSpec((1, G, LANES), lambda b, c: (b, c, 0))
    s_spec = pl.BlockSpec((1,) + s0.shape[1:], lambda b, c: (b, 0, 0, 0))
    return pl.pallas_call(
        functools.partial(_gdn_delta_body, n_chunks=G // DN_CHUNK),
        grid=(B, L // G),
        in_specs=[row_spec, row_spec, row_spec, gb_spec, s_spec],
        out_specs=[row_spec, s_spec],
        out_shape=[jax.ShapeDtypeStruct((B, L, D), F32), jax.ShapeDtypeStruct(s0.shape, F32)],
        scratch_shapes=[pltpu.VMEM(s0.shape[1:], F32)],
        compiler_params=_params(2), name="gdn_delta")(q, k, v, gb, s0.astype(F32))


def _gdn_out_body(x_ref, o_ref, z_ref, nw_ref, w_ref, out_ref, y_scr, *, nb, tl):
    tm = nb * tl
    d = x_ref.shape[-1]
    tc = MXU_N
    o = o_ref[...].reshape(tm, d)
    z = z_ref[...].reshape(tm, d)
    for h in range(d // DN_DK):
        cols = slice(h * DN_DK, (h + 1) * DN_DK)
        oh = o[:, cols]
        y_scr[:, cols] = (_rms_rows(oh, nw_ref[...]) * _silu(z[:, cols])).astype(BF16)
    y = y_scr[...]
    x = x_ref[...].reshape(tm, d)
    for j in range(d // tc):
        cols = slice(j * tc, (j + 1) * tc)
        out_ref[:, :, cols] = (x[:, cols] + _dot(y, w_ref[j])).reshape(nb, tl, tc)


def _gdn_out(x, o, z, norm_w, w_out):
    B, L, D = x.shape
    nb, tl = _tiling(B, L)
    tc = MXU_N
    w = w_out.astype(BF16).reshape(D, D // tc, tc).transpose(1, 0, 2)
    row_spec = pl.BlockSpec((nb, tl, D), lambda i, t: (i, t, 0))
    return pl.pallas_call(
        functools.partial(_gdn_out_body, nb=nb, tl=tl),
        grid=(B // nb, L // tl),
        in_specs=[row_spec, row_spec, row_spec, _const_spec((1, DN_DK)), _const_spec(w.shape)],
        out_specs=row_spec, out_shape=jax.ShapeDtypeStruct((B, L, D), F32),
        scratch_shapes=[pltpu.VMEM((nb * tl, D), BF16)],
        compiler_params=_params(2), name="gdn_out")(x, o, z, norm_w.reshape(1, DN_DK), w)


def _gated_delta_mixer(x, cbuf, s0, norm_g, w_in, conv_w, a_log, dt_bias, norm_w, w_out):
    q, k, v, z, gb, ncbuf = _gdn_in(x, cbuf, norm_g, w_in, conv_w, a_log, dt_bias)
    o, s_new = _gdn_delta(q, k, v, gb, s0)
    return _gdn_out(x, o, z, norm_w, w_out), ncbuf, s_new.astype(s0.dtype)


def _sb_qkv_body(x_ref, ng_ref, w_ref, q_ref, k_ref, v_ref, *, nb, tl):
    tm = nb * tl
    d = x_ref.shape[-1]
    tc = MXU_N
    n_sec = d // tc
    h = _rms_rows(x_ref[...].reshape(tm, d), ng_ref[...]).astype(BF16)
    outs = (q_ref, k_ref, v_ref)
    for j in range(3 * n_sec):
        sec, jc = divmod(j, n_sec)
        y = _dot(h, w_ref[j])
        if sec == 0:
            y = y * (SB_DH ** -0.5)
        outs[sec][:, :, jc * tc:(jc + 1) * tc] = y.reshape(nb, tl, tc)


def _sb_qkv(x, norm_g, w_qkv):
    B, L, D = x.shape
    nb, tl = _tiling(B, L)
    tc = MXU_N
    w = w_qkv.astype(BF16).reshape(D, 3 * D // tc, tc).transpose(1, 0, 2)
    row_spec = pl.BlockSpec((nb, tl, D), lambda i, t: (i, t, 0))
    act = jax.ShapeDtypeStruct((B, L, D), F32)
    return pl.pallas_call(
        functools.partial(_sb_qkv_body, nb=nb, tl=tl),
        grid=(B // nb, L // tl),
        in_specs=[row_spec, _const_spec((1, D)), _const_spec(w.shape)],
        out_specs=[row_spec, row_spec, row_spec], out_shape=[act, act, act],
        compiler_params=_params(2), name="sb_qkv")(x, norm_g.reshape(1, D), w)


def _sb_attn_body(q_ref, k_ref, v_ref, o_ref, kb_scr, vb_scr, u_scr, acc_scr, c_scr, *, tq, tk, rs, q_start):
    qi = pl.program_id(2)

    @pl.when(qi == 0)
    def _():
        kb_scr[...] = k_ref[0].astype(BF16)
        vb_scr[...] = v_ref[0].astype(BF16)
        ui = lax.broadcasted_iota(jnp.int32, (tk, tk), 0)
        uj = lax.broadcasted_iota(jnp.int32, (tk, tk), 1)
        u_scr[...] = (ui >= uj).astype(BF16)

    p0 = q_start + qi * tq
    n_full = p0 // tk
    acc_scr[...] = jnp.zeros_like(acc_scr)
    c_scr[...] = jnp.zeros_like(c_scr)
    lane = lax.broadcasted_iota(jnp.int32, (rs, LANES), 1)
    lo = lane < SB_DH

    def block(j, masked):
        k0 = pl.multiple_of(j * tk, tk)
        kblk = kb_scr[pl.ds(k0, tk), :]
        vblk = vb_scr[pl.ds(k0, tk), :]
        u = u_scr[...]
        for r in range(tq // rs):
            rows = slice(r * rs, (r + 1) * rs)
            q2 = q_ref[0, rows, :]
            if masked:
                qpos = p0 + r * rs + lax.broadcasted_iota(jnp.int32, (rs, tk), 0)
                kpos = k0 + lax.broadcasted_iota(jnp.int32, (rs, tk), 1)
                valid = kpos < qpos
            for hd in range(2):
                qh = jnp.where(lo if hd == 0 else ~lo, q2, 0.0).astype(BF16)
                z = _dot_nt(qh, kblk)
                sp = jnp.maximum(z, 0.0) + jnp.log(1.0 + jnp.exp(-jnp.abs(z)))
                if masked:
                    sp = jnp.where(valid, sp, 0.0)
                sp_hi = sp.astype(BF16)
                sp_lo = (sp - sp_hi.astype(F32)).astype(BF16)
                incl = _dot(sp_hi, u) + _dot(sp_lo, u)
                c = c_scr[hd, rows, :]
                e = z - incl - jnp.concatenate([c] * (tk // LANES), axis=1)
                if masked:
                    e = jnp.where(valid, e, -1e30)
                a = jnp.exp(e)
                acc_scr[hd, rows, :] += _dot(a.astype(BF16), vblk)
                c_scr[hd, rows, :] = c + jnp.broadcast_to(incl[:, 0:1], (rs, LANES))

    block(n_full, True)

    def body(i, carry):
        block(n_full - 1 - i, False)
        return carry

    lax.fori_loop(0, n_full, body, 0)
    lane_q = lax.broadcasted_iota(jnp.int32, (tq, LANES), 1)
    o_ref[0] = jnp.where(lane_q < SB_DH, acc_scr[0], acc_scr[1])


def _sb_attend(q, k_all, v_all, q_start):
    B, Lq, D = q.shape
    Lk = k_all.shape[1]
    tk = MXU_N
    tq = min(Lq, tk)
    rs = min(tq, 64)
    assert Lq % tq == 0 and Lk % tk == 0 and q_start % tk == 0 and q_start + Lq <= Lk
    q_spec = pl.BlockSpec((1, tq, LANES), lambda b, hp, i: (b, i, hp))
    kv_spec = pl.BlockSpec((1, Lk, LANES), lambda b, hp, i: (b, 0, hp))
    return pl.pallas_call(
        functools.partial(_sb_attn_body, tq=tq, tk=tk, rs=rs, q_start=q_start),
        grid=(B, D // LANES, Lq // tq),
        in_specs=[q_spec, kv_spec, kv_spec], out_specs=q_spec,
        out_shape=jax.ShapeDtypeStruct((B, Lq, D), F32),
        scratch_shapes=[pltpu.VMEM((Lk, LANES), BF16), pltpu.VMEM((Lk, LANES), BF16), pltpu.VMEM((tk, tk), BF16),
                        pltpu.VMEM((2, tq, LANES), F32), pltpu.VMEM((2, tq, LANES), F32)],
        compiler_params=_params(3), name="sb_attn")(q, k_all, v_all)


def _proj_res_body(x_ref, o_ref, w_ref, out_ref, *, nb, tl):
    tm = nb * tl
    d = x_ref.shape[-1]
    tc = MXU_N
    y = o_ref[...].reshape(tm, d).astype(BF16)
    x = x_ref[...].reshape(tm, d)
    for j in range(d // tc):
        cols = slice(j * tc, (j + 1) * tc)
        out_ref[:, :, cols] = (x[:, cols] + _dot(y, w_ref[j])).reshape(nb, tl, tc)


def _proj_res(x, o, w_out):
    B, L, D = x.shape
    nb, tl = _tiling(B, L)
    tc = MXU_N
    w = w_out.astype(BF16).reshape(D, D // tc, tc).transpose(1, 0, 2)
    row_spec = pl.BlockSpec((nb, tl, D), lambda i, t: (i, t, 0))
    return pl.pallas_call(
        functools.partial(_proj_res_body, nb=nb, tl=tl),
        grid=(B // nb, L // tl),
        in_specs=[row_spec, row_spec, _const_spec(w.shape)],
        out_specs=row_spec, out_shape=jax.ShapeDtypeStruct((B, L, D), F32),
        compiler_params=_params(2), name="proj_res")(x, o, w)


def _sb_mixer(x, k_past, v_past, norm_g, w_qkv, w_out, start):
    B, L, D = x.shape
    q, k, v = _sb_qkv(x, norm_g, w_qkv)
    if k_past is None:
        k_all, v_all = k, v
    else:
        past = k_past.shape[1]
        lk = -(-(past + L) // MXU_N) * MXU_N
        tail = jnp.zeros((B, lk - past - L, D), F32)
        k_all = jnp.concatenate([k_past.reshape(B, past, D).astype(F32), k, tail], axis=1)
        v_all = jnp.concatenate([v_past.reshape(B, past, D).astype(F32), v, tail], axis=1)
    o = _sb_attend(q, k_all, v_all, start)
    return (_proj_res(x, o, w_out), k.reshape(B, L, SB_HEADS, SB_DH), v.reshape(B, L, SB_HEADS, SB_DH))


def _trunk(x, pool_bufs, dn_conv_bufs, dn_states, sb_k_past, sb_v_past, ffn_bufs, start, p):
    depth = p['mix_norm'].shape[0]
    n_pool, n_dnc, n_dn, n_k, n_v, n_ffn = [], [], [], [], [], []
    y = None
    for i in range(depth):
        kind, j = i % 3, i // 3
        if kind == 0:
            x, buf = _pool_mixer(x, pool_bufs[j], p['mix_norm'][i], p['pool_w'][j], p['pool_scale'][j], start)
            n_pool.append(buf)
        elif kind == 1:
            x, cbuf, s = _gated_delta_mixer(x, dn_conv_bufs[j], dn_states[j], p['mix_norm'][i], p['dn_w_in'][j],
                                            p['dn_conv_w'][j], p['dn_a_log'][j], p['dn_dt_bias'][j],
                                            p['dn_norm'][j], p['dn_w_out'][j])
            n_dnc.append(cbuf)
            n_dn.append(s)
        else:
            kp = None if sb_k_past is None else sb_k_past[j]
            vp = None if sb_v_past is None else sb_v_past[j]
            x, kn, vn = _sb_mixer(x, kp, vp, p['mix_norm'][i], p['sb_w_qkv'][j], p['sb_w_out'][j], start)
            n_k.append(kn)
            n_v.append(vn)
        fg = p['final_norm'] if i == depth - 1 else None
        x, fbuf, y = _conv_ffn(x, ffn_bufs[i], p['ffn_norm'][i], p['ffn_w_up'][i], p['ffn_conv_w'][i],
                               p['ffn_conv_b'][i], p['ffn_w_down'][i], fg)
        n_ffn.append(fbuf)
    return (y, jnp.stack(n_pool), jnp.stack(n_dnc), jnp.stack(n_dn), jnp.stack(n_k), jnp.stack(n_v),
            jnp.stack(n_ffn))


def kernel(x_prompt, x_sample, state_pool, state_dn_conv, state_dn, cache_sb_k, cache_sb_v, state_ffn_conv,
           mix_norm, ffn_norm, final_norm, pool_w, pool_scale, dn_w_in, dn_conv_w, dn_a_log, dn_dt_bias,
           dn_norm, dn_w_out, sb_w_qkv, sb_w_out, ffn_w_up, ffn_conv_w, ffn_conv_b, ffn_w_down):
    p = dict(mix_norm=mix_norm, ffn_norm=ffn_norm, final_norm=final_norm, pool_w=pool_w, pool_scale=pool_scale,
             dn_w_in=dn_w_in, dn_conv_w=dn_conv_w, dn_a_log=dn_a_log, dn_dt_bias=dn_dt_bias, dn_norm=dn_norm,
             dn_w_out=dn_w_out, sb_w_qkv=sb_w_qkv, sb_w_out=sb_w_out, ffn_w_up=ffn_w_up, ffn_conv_w=ffn_conv_w,
             ffn_conv_b=ffn_conv_b, ffn_w_down=ffn_w_down)
    bp = x_prompt.shape[0]
    dtp = x_prompt.dtype
    zero_pool = jnp.zeros((state_pool.shape[0], bp) + state_pool.shape[2:], dtp)
    zero_dnc = jnp.zeros((state_dn_conv.shape[0], bp) + state_dn_conv.shape[2:], dtp)
    zero_dn = jnp.zeros((state_dn.shape[0], bp) + state_dn.shape[2:], state_dn.dtype)
    zero_ffn = jnp.zeros((state_ffn_conv.shape[0], bp) + state_ffn_conv.shape[2:], dtp)
    y_prompt, pool_p, dnc_p, dn_p, k_p, v_p, ffn_p = _trunk(
        x_prompt, zero_pool, zero_dnc, zero_dn, None, None, zero_ffn, 0, p)
    past_len = cache_sb_k.shape[2]
    y_sample, pool_s, dnc_s, dn_s, k_s, v_s, ffn_s = _trunk(
        x_sample, state_pool, state_dn_conv, state_dn, cache_sb_k, cache_sb_v, state_ffn_conv, past_len, p)
    return (y_prompt, y_sample, pool_p, pool_s, dnc_p, dnc_s, dn_p, dn_s, k_p, k_s, v_p, v_s, ffn_p, ffn_s)
```

```python
import functools

import jax
import jax.numpy as jnp
from jax import lax
from jax.experimental import pallas as pl
from jax.experimental.pallas import tpu as pltpu

F32 = jnp.float32
BF16 = jnp.bfloat16

RMS_EPS = 1e-6
POOL_WINDOWS = (2, 4, 8, 16)
POOL_BUF = max(POOL_WINDOWS) - 1
DN_HEADS = 8
DN_DK = 128
DN_CONV = 4
DN_CHUNK = 64
SB_HEADS = 16
SB_DH = 64
FFN_CONV = 3

LANES = 128
MXU_N = 256
VMEM_LIMIT = 56 * 1024 * 1024
ROW_TILE = 512


def _params(n_axes, vmem=VMEM_LIMIT):
    return pltpu.CompilerParams(dimension_semantics=("arbitrary",) * n_axes, vmem_limit_bytes=vmem)


def _const_spec(shape):
    nd = len(shape)
    return pl.BlockSpec(shape, lambda *_: (0,) * nd, pipeline_mode=pl.Buffered(1))


def _rms_rows(x, g):
    ms = jnp.mean(x * x, axis=-1, keepdims=True)
    return x * lax.rsqrt(ms + RMS_EPS) * g


def _silu(x):
    return x * jax.nn.sigmoid(x)


def _softplus(x):
    return jnp.maximum(x, 0.0) + jnp.log1p(jnp.exp(-jnp.abs(x)))


def _dot(a, b):
    return jnp.dot(a, b, preferred_element_type=F32)


def _dot_nt(a, b, precision=None):
    return lax.dot_general(a, b, (((1,), (1,)), ((), ())), preferred_element_type=F32, precision=precision)


def _tiling(B, L):
    tl = min(L, ROW_TILE)
    nb = max(1, min(B, ROW_TILE // tl))
    assert L % tl == 0 and B % nb == 0
    return nb, tl


def _ffn_body(*refs, nb, tl, n_chunks, final):
    (x_ref, buf_ref, ng_ref, wv_ref, wg_ref, cw_ref, cb_ref, wd_ref) = refs[:8]
    k = 8
    if final:
        fg_ref = refs[k]
        k += 1
    o_ref, nbuf_ref = refs[k], refs[k + 1]
    k += 2
    if final:
        y_ref = refs[k]
        k += 1
    h_scr, acc_scr, act_scr, cv_scr, carry_scr = refs[k:]
    t = pl.program_id(1)
    tm = nb * tl
    d = x_ref.shape[-1]
    pad = 8

    x = x_ref[...].reshape(tm, d)
    h_scr[...] = _rms_rows(x, ng_ref[...]).astype(BF16)
    acc_scr[...] = jnp.zeros_like(acc_scr)

    @pl.when(t == 0)
    def _():
        carry_scr[...] = buf_ref[...]

    def chunk(j, c):
        h = h_scr[...]
        hv = _dot(h, wv_ref[j])
        hg = _dot(h, wg_ref[j])
        cw = cw_ref[j]
        cb = cb_ref[j]
        for b in range(nb):
            gb = hg[b * tl:(b + 1) * tl]
            cv_scr[b, pad - 2:pad, :] = carry_scr[b, j]
            cv_scr[b, pad:pad + tl, :] = gb
            conv = (cv_scr[b, pad - 2:pad - 2 + tl, :] * cw[0:1] + cv_scr[b, pad - 1:pad - 1 + tl, :] * cw[1:2]
                    + gb * cw[2:3])
            last = gb[tl - 2:tl]
            carry_scr[b, j] = last
            nbuf_ref[b, j] = last
            act = _silu(conv + cb) * hv[b * tl:(b + 1) * tl]
            act_scr[b * tl:(b + 1) * tl, :] = act.astype(BF16)
        acc_scr[...] += _dot(act_scr[...], wd_ref[j])
        return c

    lax.fori_loop(0, n_chunks, chunk, 0)
    out = x_ref[...].reshape(tm, d) + acc_scr[...]
    o_ref[...] = out.reshape(nb, tl, d)
    if final:
        y_ref[...] = _rms_rows(out, fg_ref[...]).reshape(nb, tl, d)


def _conv_ffn(x, buf, norm_g, w_up, conv_w, conv_b, w_down, final_g=None):
    B, L, D = x.shape
    F = w_down.shape[0]
    tf = MXU_N
    n_chunks = F // tf
    assert F % tf == 0
    nb, tl = _tiling(B, L)
    tm = nb * tl
    final = final_g is not None

    wv = w_up[:, :F].astype(BF16).reshape(D, n_chunks, tf).transpose(1, 0, 2)
    wg = w_up[:, F:].astype(BF16).reshape(D, n_chunks, tf).transpose(1, 0, 2)
    wd = w_down.astype(BF16).reshape(n_chunks, tf, D)
    cw = conv_w.reshape(FFN_CONV, n_chunks, tf).transpose(1, 0, 2)
    cb = conv_b.reshape(n_chunks, 1, tf)
    bufc = buf.astype(F32).reshape(B, FFN_CONV - 1, n_chunks, tf).transpose(0, 2, 1, 3)

    row_spec = pl.BlockSpec((nb, tl, D), lambda i, t: (i, t, 0))
    buf_spec = pl.BlockSpec((nb, n_chunks, FFN_CONV - 1, tf), lambda i, t: (i, 0, 0, 0))
    in_specs = [row_spec, buf_spec, _const_spec((1, D)), _const_spec(wv.shape), _const_spec(wg.shape),
                _const_spec(cw.shape), _const_spec(cb.shape), _const_spec(wd.shape)]
    args = [x, bufc, norm_g.reshape(1, D), wv, wg, cw, cb, wd]
    out_shape = [jax.ShapeDtypeStruct((B, L, D), F32),
                 jax.ShapeDtypeStruct((B, n_chunks, FFN_CONV - 1, tf), F32)]
    out_specs = [row_spec, buf_spec]
    if final:
        in_specs.append(_const_spec((1, D)))
        args.append(final_g.reshape(1, D))
        out_shape.append(jax.ShapeDtypeStruct((B, L, D), F32))
        out_specs.append(row_spec)
    scratch = [pltpu.VMEM((tm, D), BF16), pltpu.VMEM((tm, D), F32), pltpu.VMEM((tm, tf), BF16),
               pltpu.VMEM((nb, tl + 8, tf), F32), pltpu.VMEM((nb, n_chunks, FFN_CONV - 1, tf), F32)]
    outs = pl.pallas_call(
        functools.partial(_ffn_body, nb=nb, tl=tl, n_chunks=n_chunks, final=final),
        grid=(B // nb, L // tl), in_specs=in_specs, out_specs=out_specs, out_shape=out_shape,
        scratch_shapes=scratch, compiler_params=_params(2), name="conv_ffn")(*args)
    nbuf = outs[1].transpose(0, 2, 1, 3).reshape(B, FFN_CONV - 1, F)
    return outs[0], nbuf, (outs[2] if final else None)


def _pool_body(x_ref, buf_ref, ng_ref, w_ref, sc_ref, o_ref, nbuf_ref, ext_scr, *, nb, tl, start):
    t = pl.program_id(1)
    d = x_ref.shape[-1]
    hist = POOL_BUF + 1
    gc = d // len(POOL_WINDOWS)
    row = lax.broadcasted_iota(jnp.int32, (tl, 1), 0)
    n_avail = start + t * tl + row + 1
    for b in range(nb):
        @pl.when(t == 0)
        def _():
            ext_scr[b, 0:hist, :] = jnp.zeros((hist, d), F32)
            ext_scr[b, 1:hist, :] = buf_ref[b]

        @pl.when(t > 0)
        def _():
            ext_scr[b, 0:hist, :] = ext_scr[b, tl:tl + hist, :]

        x = x_ref[b]
        h = _rms_rows(x, ng_ref[...])
        ext_scr[b, hist:hist + tl, :] = h
        for g, win in enumerate(POOL_WINDOWS):
            cols = slice(g * gc, (g + 1) * gc)
            s = h[:, cols]
            for i in range(1, win):
                s = s + ext_scr[b, hist - i:hist - i + tl, cols]
            cnt = jnp.minimum(n_avail, win).astype(F32)
            dmean = s * (1.0 / cnt) - h[:, cols]
            y = _dot(dmean.astype(BF16), w_ref[g])
            o_ref[b, :, cols] = x[:, cols] + y * sc_ref[:, cols]
        nbuf_ref[b] = ext_scr[b, tl + 1:tl + hist, :]


def _pool_mixer(x, buf, norm_g, w, scale, start):
    B, L, D = x.shape
    nb, tl = _tiling(B, L)
    row_spec = pl.BlockSpec((nb, tl, D), lambda i, t: (i, t, 0))
    buf_spec = pl.BlockSpec((nb, POOL_BUF, D), lambda i, t: (i, 0, 0))
    wb = w.astype(BF16)
    return pl.pallas_call(
        functools.partial(_pool_body, nb=nb, tl=tl, start=start),
        grid=(B // nb, L // tl),
        in_specs=[row_spec, buf_spec, _const_spec((1, D)), _const_spec(wb.shape), _const_spec((1, D))],
        out_specs=[row_spec, buf_spec],
        out_shape=[jax.ShapeDtypeStruct((B, L, D), F32), jax.ShapeDtypeStruct((B, POOL_BUF, D), F32)],
        scratch_shapes=[pltpu.VMEM((nb, tl + POOL_BUF + 1, D), F32)],
        compiler_params=_params(2), name="pool_mixer")(x, buf.astype(F32), norm_g.reshape(1, D), wb,
                                                      scale.reshape(1, D))


def _gdn_in_body(x_ref, cbuf_ref, ng_ref, wqkv_ref, cw_ref, wz_ref, wab_ref, alog_ref, dtb_ref,
                 q_ref, k_ref, v_ref, z_ref, gb_ref, ncbuf_ref, h_scr, cv_scr, carry_scr, *, nb, tl):
    t = pl.program_id(1)
    tm = nb * tl
    d = x_ref.shape[-1]
    pad = 8
    tc = MXU_N
    n_sec = d // tc
    hw = DN_CONV - 1

    x = x_ref[...].reshape(tm, d)
    h_scr[...] = _rms_rows(x, ng_ref[...]).astype(BF16)
    h = h_scr[...]

    @pl.when(t == 0)
    def _():
        carry_scr[...] = cbuf_ref[...]

    outs = (q_ref, k_ref, v_ref)
    for j in range(3 * n_sec):
        sec, jc = divmod(j, n_sec)
        y = _dot(h, wqkv_ref[j])
        cw = cw_ref[j]
        for b in range(nb):
            yb = y[b * tl:(b + 1) * tl]
            cv_scr[b, pad - hw:pad, :] = carry_scr[b, j]
            cv_scr[b, pad:pad + tl, :] = yb
            conv = yb * cw[hw:hw + 1]
            for i in range(hw):
                conv = conv + cv_scr[b, pad - hw + i:pad - hw + i + tl, :] * cw[i:i + 1]
            last = yb[tl - hw:tl]
            carry_scr[b, j] = last
            ncbuf_ref[b, j] = last
            s = _silu(conv)
            for half in range(tc // DN_DK):
                sh = s[:, half * DN_DK:(half + 1) * DN_DK]
                if sec < 2:
                    sh = sh * lax.rsqrt(jnp.sum(sh * sh, axis=-1, keepdims=True) + RMS_EPS)
                if sec == 0:
                    sh = sh * (DN_DK ** -0.5)
                c0 = jc * tc + half * DN_DK
                outs[sec][b, :, c0:c0 + DN_DK] = sh
    for j in range(n_sec):
        z_ref[:, :, j * tc:(j + 1) * tc] = _dot(h, wz_ref[j]).reshape(nb, tl, tc)
    ab = _dot(h, wab_ref[...])
    g = -jnp.exp(alog_ref[...]) * _softplus(ab + dtb_ref[...])
    lane = lax.broadcasted_iota(jnp.int32, ab.shape, 1)
    gb_ref[...] = jnp.where(lane < DN_HEADS, g, jax.nn.sigmoid(ab)).reshape(nb, tl, LANES)


def _gdn_in(x, cbuf, norm_g, w_in, conv_w, a_log, dt_bias):
    B, L, D = x.shape
    nb, tl = _tiling(B, L)
    tm = nb * tl
    tc = MXU_N
    qkv_w = 3 * D
    n_qkv = qkv_w // tc
    n_sec = D // tc
    wqkv = w_in[:, :qkv_w].astype(BF16).reshape(D, n_qkv, tc).transpose(1, 0, 2)
    wz = w_in[:, qkv_w:qkv_w + D].astype(BF16).reshape(D, n_sec, tc).transpose(1, 0, 2)
    wab = jnp.zeros((D, LANES), BF16).at[:, :2 * DN_HEADS].set(w_in[:, qkv_w + D:].astype(BF16))
    cw = conv_w.reshape(DN_CONV, n_qkv, tc).transpose(1, 0, 2)
    alog = jnp.zeros((1, LANES), F32).at[0, :DN_HEADS].set(a_log)
    dtb = jnp.zeros((1, LANES), F32).at[0, :DN_HEADS].set(dt_bias)
    cbufc = cbuf.astype(F32).reshape(B, DN_CONV - 1, n_qkv, tc).transpose(0, 2, 1, 3)

    row_spec = pl.BlockSpec((nb, tl, D), lambda i, t: (i, t, 0))
    gb_spec = pl.BlockSpec((nb, tl, LANES), lambda i, t: (i, t, 0))
    cb_spec = pl.BlockSpec((nb, n_qkv, DN_CONV - 1, tc), lambda i, t: (i, 0, 0, 0))
    act = jax.ShapeDtypeStruct((B, L, D), F32)
    outs = pl.pallas_call(
        functools.partial(_gdn_in_body, nb=nb, tl=tl),
        grid=(B // nb, L // tl),
        in_specs=[row_spec, cb_spec, _const_spec((1, D)), _const_spec(wqkv.shape), _const_spec(cw.shape),
                  _const_spec(wz.shape), _const_spec(wab.shape), _const_spec((1, LANES)),
                  _const_spec((1, LANES))],
        out_specs=[row_spec, row_spec, row_spec, row_spec, gb_spec, cb_spec],
        out_shape=[act, act, act, act, jax.ShapeDtypeStruct((B, L, LANES), F32),
                   jax.ShapeDtypeStruct((B, n_qkv, DN_CONV - 1, tc), F32)],
        scratch_shapes=[pltpu.VMEM((tm, D), BF16), pltpu.VMEM((nb, tl + 8, tc), F32),
                        pltpu.VMEM((nb, n_qkv, DN_CONV - 1, tc), F32)],
        compiler_params=_params(2), name="gdn_in")(x, cbufc, norm_g.reshape(1, D), wqkv, cw, wz, wab, alog, dtb)
    q, k, v, z, gb, ncbuf = outs
    return q, k, v, z, gb, ncbuf.transpose(0, 2, 1, 3).reshape(B, DN_CONV - 1, qkv_w)


def _gdn_delta_body(q_ref, k_ref, v_ref, gb_ref, s0_ref, o_ref, sout_ref, s_scr, *, n_chunks):
    c_idx = pl.program_id(1)
    C = DN_CHUNK
    dk = DN_DK
    hp = lax.Precision.HIGHEST

    @pl.when(c_idx == 0)
    def _():
        s_scr[...] = s0_ref[0]

    ii = lax.broadcasted_iota(jnp.int32, (C, C), 0)
    jj = lax.broadcasted_iota(jnp.int32, (C, C), 1)
    lower_incl = ii >= jj
    lower_strict = ii > jj
    tri = lower_incl.astype(F32)
    eye = (ii == jj).astype(F32)
    zpad = jnp.zeros((LANES - C, LANES), F32)

    def chunk(ci, carry):
        r0 = pl.multiple_of(ci * C, C)
        rows = pl.ds(r0, C)
        gb = gb_ref[0, rows, :]
        gcs = jnp.dot(tri, gb, preferred_element_type=F32, precision=hp)
        gcs_t = jnp.concatenate([gcs, zpad], axis=0).T
        heads = range(DN_HEADS)
        tinv, qpow, attn, rhs, qg, kg_t, glast = [], [], [], [], [], [], []
        for h in heads:
            cols = slice(h * dk, (h + 1) * dk)
            q = q_ref[0, rows, cols]
            k = k_ref[0, rows, cols]
            v = v_ref[0, rows, cols]
            gcol = gcs[:, h:h + 1]
            grow = gcs_t[h:h + 1, 0:C]
            bcol = gb[:, DN_HEADS + h:DN_HEADS + h + 1]
            diff = jnp.where(lower_incl, gcol - grow, 0.0)
            dec_incl = jnp.where(lower_incl, jnp.exp(diff), 0.0)
            kb = k * bcol
            mq = _dot_nt(jnp.concatenate([kb, q], axis=0).astype(BF16), k.astype(BF16))
            p = jnp.where(lower_strict, -mq[:C] * dec_incl, 0.0)
            attn.append(mq[C:] * dec_incl)
            tinv.append(eye + p)
            qpow.append(p)
            egc = jnp.exp(gcol)
            rhs.append(jnp.concatenate([v * bcol, kb * egc], axis=1).astype(BF16))
            qg.append(q * egc)
            gl = grow[:, C - 1:C]
            glast.append(gl)
            k_t = jnp.concatenate([k, zpad], axis=0).T[:, 0:C]
            kg_t.append(k_t * jnp.exp(gl - grow))
        for h in heads:
            pb = qpow[h].astype(BF16)
            qpow[h] = _dot(pb, pb)
        for step in range(5):
            for h in heads:
                qb = qpow[h].astype(BF16)
                if step < 4:
                    tq2 = _dot(jnp.concatenate([tinv[h], qpow[h]], axis=0).astype(BF16), qb)
                    tinv[h] = tinv[h] + tq2[:C]
                    qpow[h] = tq2[C:]
                else:
                    tinv[h] = tinv[h] + _dot(tinv[h].astype(BF16), qb)
        uw = [_dot(tinv[h].astype(BF16), rhs[h]) for h in heads]
        s_old = [s_scr[h] for h in heads]
        ws_qs = [_dot(jnp.concatenate([uw[h][:, dk:], qg[h]], axis=0).astype(BF16), s_old[h].astype(BF16))
                 for h in heads]
        for h in heads:
            v_new = uw[h][:, :dk] - ws_qs[h][:C]
            x = _dot(jnp.concatenate([attn[h], kg_t[h]], axis=0).astype(BF16), v_new.astype(BF16))
            o_ref[0, rows, h * dk:(h + 1) * dk] = ws_qs[h][C:] + x[:C]
            s_scr[h] = s_old[h] * jnp.exp(glast[h]) + x[C:]
        return carry

    lax.fori_loop(0, n_chunks, chunk, 0)
    sout_ref[0] = s_scr[...]


def _gdn_delta(q, k, v, gb, s0):
    B, L, D = q.shape
    G = min(L, 1024)
    assert L % G == 0 and G % DN_CHUNK == 0
    row_spec = pl.BlockSpec((1, G, D), lambda b, c: (b, c, 0))
    gb_spec = pl.BlockSpec((1, G, LANES), lambda b, c: (b, c, 0))
    s_spec = pl.BlockSpec((1,) + s0.shape[1:], lambda b, c: (b, 0, 0, 0))
    return pl.pallas_call(
        functools.partial(_gdn_delta_body, n_chunks=G // DN_CHUNK),
        grid=(B, L // G),
        in_specs=[row_spec, row_spec, row_spec, gb_spec, s_spec],
        out_specs=[row_spec, s_spec],
        out_shape=[jax.ShapeDtypeStruct((B, L, D), F32), jax.ShapeDtypeStruct(s0.shape, F32)],
        scratch_shapes=[pltpu.VMEM(s0.shape[1:], F32)],
        compiler_params=_params(2), name="gdn_delta")(q, k, v, gb, s0.astype(F32))


def _gdn_out_body(x_ref, o_ref, z_ref, nw_ref, w_ref, out_ref, y_scr, *, nb, tl):
    tm = nb * tl
    d = x_ref.shape[-1]
    tc = MXU_N
    o = o_ref[...].reshape(tm, d)
    z = z_ref[...].reshape(tm, d)
    for h in range(d // DN_DK):
        cols = slice(h * DN_DK, (h + 1) * DN_DK)
        oh = o[:, cols]
        y_scr[:, cols] = (_rms_rows(oh, nw_ref[...]) * _silu(z[:, cols])).astype(BF16)
    y = y_scr[...]
    x = x_ref[...].reshape(tm, d)
    for j in range(d // tc):
        cols = slice(j * tc, (j + 1) * tc)
        out_ref[:, :, cols] = (x[:, cols] + _dot(y, w_ref[j])).reshape(nb, tl, tc)


def _gdn_out(x, o, z, norm_w, w_out):
    B, L, D = x.shape
    nb, tl = _tiling(B, L)
    tc = MXU_N
    w = w_out.astype(BF16).reshape(D, D // tc, tc).transpose(1, 0, 2)
    row_spec = pl.BlockSpec((nb, tl, D), lambda i, t: (i, t, 0))
    return pl.pallas_call(
        functools.partial(_gdn_out_body, nb=nb, tl=tl),
        grid=(B // nb, L // tl),
        in_specs=[row_spec, row_spec, row_spec, _const_spec((1, DN_DK)), _const_spec(w.shape)],
        out_specs=row_spec, out_shape=jax.ShapeDtypeStruct((B, L, D), F32),
        scratch_shapes=[pltpu.VMEM((nb * tl, D), BF16)],
        compiler_params=_params(2), name="gdn_out")(x, o, z, norm_w.reshape(1, DN_DK), w)


def _gated_delta_mixer(x, cbuf, s0, norm_g, w_in, conv_w, a_log, dt_bias, norm_w, w_out):
    q, k, v, z, gb, ncbuf = _gdn_in(x, cbuf, norm_g, w_in, conv_w, a_log, dt_bias)
    o, s_new = _gdn_delta(q, k, v, gb, s0)
    return _gdn_out(x, o, z, norm_w, w_out), ncbuf, s_new.astype(s0.dtype)


def _sb_qkv_body(x_ref, ng_ref, w_ref, q_ref, k_ref, v_ref, *, nb, tl):
    tm = nb * tl
    d = x_ref.shape[-1]
    tc = MXU_N
    n_sec = d // tc
    h = _rms_rows(x_ref[...].reshape(tm, d), ng_ref[...]).astype(BF16)
    outs = (q_ref, k_ref, v_ref)
    for j in range(3 * n_sec):
        sec, jc = divmod(j, n_sec)
        y = _dot(h, w_ref[j])
        if sec == 0:
            y = y * (SB_DH ** -0.5)
        outs[sec][:, :, jc * tc:(jc + 1) * tc] = y.reshape(nb, tl, tc)


def _sb_qkv(x, norm_g, w_qkv):
    B, L, D = x.shape
    nb, tl = _tiling(B, L)
    tc = MXU_N
    w = w_qkv.astype(BF16).reshape(D, 3 * D // tc, tc).transpose(1, 0, 2)
    row_spec = pl.BlockSpec((nb, tl, D), lambda i, t: (i, t, 0))
    act = jax.ShapeDtypeStruct((B, L, D), F32)
    return pl.pallas_call(
        functools.partial(_sb_qkv_body, nb=nb, tl=tl),
        grid=(B // nb, L // tl),
        in_specs=[row_spec, _const_spec((1, D)), _const_spec(w.shape)],
        out_specs=[row_spec, row_spec, row_spec], out_shape=[act, act, act],
        compiler_params=_params(2), name="sb_qkv")(x, norm_g.reshape(1, D), w)


def _sb_attn_body(q_ref, k_ref, v_ref, o_ref, kb_scr, vb_scr, u_scr, qs_scr, acc_scr, c_scr, *, tq, tk, q_start):
    qi = pl.program_id(2)

    @pl.when(qi == 0)
    def _():
        kb_scr[...] = k_ref[0].astype(BF16)
        vb_scr[...] = v_ref[0].astype(BF16)
        ui = lax.broadcasted_iota(jnp.int32, (tk, tk), 0)
        uj = lax.broadcasted_iota(jnp.int32, (tk, tk), 1)
        u_scr[...] = (ui >= uj).astype(BF16)

    p0 = q_start + qi * tq
    n_full = p0 // tk
    q2 = q_ref[0]
    lo = lax.broadcasted_iota(jnp.int32, (tq, LANES), 1) < SB_DH
    qs_scr[0:tq, :] = jnp.where(lo, q2, 0.0).astype(BF16)
    qs_scr[tq:2 * tq, :] = jnp.where(lo, 0.0, q2).astype(BF16)
    acc_scr[...] = jnp.zeros_like(acc_scr)
    c_scr[...] = jnp.zeros_like(c_scr)

    def block(j, masked):
        k0 = pl.multiple_of(j * tk, tk)
        z = _dot_nt(qs_scr[...], kb_scr[pl.ds(k0, tk), :])
        sp = jnp.maximum(z, 0.0) + jnp.log(1.0 + jnp.exp(-jnp.abs(z)))
        if masked:
            row = lax.broadcasted_iota(jnp.int32, (2 * tq, tk), 0)
            qpos = p0 + (row & (tq - 1))
            kpos = k0 + lax.broadcasted_iota(jnp.int32, (2 * tq, tk), 1)
            valid = kpos < qpos
            sp = jnp.where(valid, sp, 0.0)
        incl = _dot(sp.astype(BF16), u_scr[...])
        c = c_scr[...]
        e = z - incl - jnp.concatenate([c] * (tk // LANES), axis=1)
        if masked:
            e = jnp.where(valid, e, -1e30)
        a = jnp.exp(e)
        acc_scr[...] += _dot(a.astype(BF16), vb_scr[pl.ds(k0, tk), :])
        c_scr[...] = c + jnp.broadcast_to(incl[:, 0:1], (2 * tq, LANES))

    block(n_full, True)

    def body(i, carry):
        block(n_full - 1 - i, False)
        return carry

    lax.fori_loop(0, n_full, body, 0)
    o_ref[0] = jnp.where(lo, acc_scr[0:tq, :], acc_scr[tq:2 * tq, :])


def _sb_attend(q, k_all, v_all, q_start):
    B, Lq, D = q.shape
    Lk = k_all.shape[1]
    tk = MXU_N
    tq = min(Lq, tk)
    assert Lq % tq == 0 and Lk % tk == 0 and q_start % tk == 0 and q_start + Lq <= Lk and tq & (tq - 1) == 0
    q_spec = pl.BlockSpec((1, tq, LANES), lambda b, hp, i: (b, i, hp))
    kv_spec = pl.BlockSpec((1, Lk, LANES), lambda b, hp, i: (b, 0, hp))
    return pl.pallas_call(
        functools.partial(_sb_attn_body, tq=tq, tk=tk, q_start=q_start),
        grid=(B, D // LANES, Lq // tq),
        in_specs=[q_spec, kv_spec, kv_spec], out_specs=q_spec,
        out_shape=jax.ShapeDtypeStruct((B, Lq, D), F32),
        scratch_shapes=[pltpu.VMEM((Lk, LANES), BF16), pltpu.VMEM((Lk, LANES), BF16), pltpu.VMEM((tk, tk), BF16),
                        pltpu.VMEM((2 * tq, LANES), BF16), pltpu.VMEM((2 * tq, LANES), F32),
                        pltpu.VMEM((2 * tq, LANES), F32)],
        compiler_params=_params(3), name="sb_attn")(q, k_all, v_all)


def _proj_res_body(x_ref, o_ref, w_ref, out_ref, *, nb, tl):
    tm = nb * tl
    d = x_ref.shape[-1]
    tc = MXU_N
    y = o_ref[...].reshape(tm, d).astype(BF16)
    x = x_ref[...].reshape(tm, d)
    for j in range(d // tc):
        cols = slice(j * tc, (j + 1) * tc)
        out_ref[:, :, cols] = (x[:, cols] + _dot(y, w_ref[j])).reshape(nb, tl, tc)


def _proj_res(x, o, w_out):
    B, L, D = x.shape
    nb, tl = _tiling(B, L)
    tc = MXU_N
    w = w_out.astype(BF16).reshape(D, D // tc, tc).transpose(1, 0, 2)
    row_spec = pl.BlockSpec((nb, tl, D), lambda i, t: (i, t, 0))
    return pl.pallas_call(
        functools.partial(_proj_res_body, nb=nb, tl=tl),
        grid=(B // nb, L // tl),
        in_specs=[row_spec, row_spec, _const_spec(w.shape)],
        out_specs=row_spec, out_shape=jax.ShapeDtypeStruct((B, L, D), F32),
        compiler_params=_params(2), name="proj_res")(x, o, w)


def _sb_mixer(x, k_past, v_past, norm_g, w_qkv, w_out, start):
    B, L, D = x.shape
    q, k, v = _sb_qkv(x, norm_g, w_qkv)
    if k_past is None:
        k_all, v_all = k, v
    else:
        past = k_past.shape[1]
        lk = -(-(past + L) // MXU_N) * MXU_N
        tail = jnp.zeros((B, lk - past - L, D), F32)
        k_all = jnp.concatenate([k_past.reshape(B, past, D).astype(F32), k, tail], axis=1)
        v_all = jnp.concatenate([v_past.reshape(B, past, D).astype(F32), v, tail], axis=1)
    o = _sb_attend(q, k_all, v_all, start)
    return (_proj_res(x, o, w_out), k.reshape(B, L, SB_HEADS, SB_DH), v.reshape(B, L, SB_HEADS, SB_DH))


def _trunk(x, pool_bufs, dn_conv_bufs, dn_states, sb_k_past, sb_v_past, ffn_bufs, start, p):
    depth = p['mix_norm'].shape[0]
    n_pool, n_dnc, n_dn, n_k, n_v, n_ffn = [], [], [], [], [], []
    y = None
    for i in range(depth):
        kind, j = i % 3, i // 3
        if kind == 0:
            x, buf = _pool_mixer(x, pool_bufs[j], p['mix_norm'][i], p['pool_w'][j], p['pool_scale'][j], start)
            n_pool.append(buf)
        elif kind == 1:
            x, cbuf, s = _gated_delta_mixer(x, dn_conv_bufs[j], dn_states[j], p['mix_norm'][i], p['dn_w_in'][j],
                                            p['dn_conv_w'][j], p['dn_a_log'][j], p['dn_dt_bias'][j],
                                            p['dn_norm'][j], p['dn_w_out'][j])
            n_dnc.append(cbuf)
            n_dn.append(s)
        else:
            kp = None if sb_k_past is None else sb_k_past[j]
            vp = None if sb_v_past is None else sb_v_past[j]
            x, kn, vn = _sb_mixer(x, kp, vp, p['mix_norm'][i], p['sb_w_qkv'][j], p['sb_w_out'][j], start)
            n_k.append(kn)
            n_v.append(vn)
        fg = p['final_norm'] if i == depth - 1 else None
        x, fbuf, y = _conv_ffn(x, ffn_bufs[i], p['ffn_norm'][i], p['ffn_w_up'][i], p['ffn_conv_w'][i],
                               p['ffn_conv_b'][i], p['ffn_w_down'][i], fg)
        n_ffn.append(fbuf)
    return (y, jnp.stack(n_pool), jnp.stack(n_dnc), jnp.stack(n_dn), jnp.stack(n_k), jnp.stack(n_v),
            jnp.stack(n_ffn))


def kernel(x_prompt, x_sample, state_pool, state_dn_conv, state_dn, cache_sb_k, cache_sb_v, state_ffn_conv,
           mix_norm, ffn_norm, final_norm, pool_w, pool_scale, dn_w_in, dn_conv_w, dn_a_log, dn_dt_bias,
           dn_norm, dn_w_out, sb_w_qkv, sb_w_out, ffn_w_up, ffn_conv_w, ffn_conv_b, ffn_w_down):
    p = dict(mix_norm=mix_norm, ffn_norm=ffn_norm, final_norm=final_norm, pool_w=pool_w, pool_scale=pool_scale,
             dn_w_in=dn_w_in, dn_conv_w=dn_conv_w, dn_a_log=dn_a_log, dn_dt_bias=dn_dt_bias, dn_norm=dn_norm,
             dn_w_out=dn_w_out, sb_w_qkv=sb_w_qkv, sb_w_out=sb_w_out, ffn_w_up=ffn_w_up, ffn_conv_w=ffn_conv_w,
             ffn_conv_b=ffn_conv_b, ffn_w_down=ffn_w_down)
    bp = x_prompt.shape[0]
    dtp = x_prompt.dtype
    zero_pool = jnp.zeros((state_pool.shape[0], bp) + state_pool.shape[2:], dtp)
    zero_dnc = jnp.zeros((state_dn_conv.shape[0], bp) + state_dn_conv.shape[2:], dtp)
    zero_dn = jnp.zeros((state_dn.shape[0], bp) + state_dn.shape[2:], state_dn.dtype)
    zero_ffn = jnp.zeros((state_ffn_conv.shape[0], bp) + state_ffn_conv.shape[2:], dtp)
    y_prompt, pool_p, dnc_p, dn_p, k_p, v_p, ffn_p = _trunk(
        x_prompt, zero_pool, zero_dnc, zero_dn, None, None, zero_ffn, 0, p)
    past_len = cache_sb_k.shape[2]
    y_sample, pool_s, dnc_s, dn_s, k_s, v_s, ffn_s = _trunk(
        x_sample, state_pool, state_dn_conv, state_dn, cache_sb_k, cache_sb_v, state_ffn_conv, past_len, p)
    return (y_prompt, y_sample, pool_p, pool_s, dnc_p, dnc_s, dn_p, dn_s, k_p, k_s, v_p, v_s, ffn_p, ffn_s)
```

```python
import functools

import jax
import jax.numpy as jnp
from jax import lax
from jax.experimental import pallas as pl
from jax.experimental.pallas import tpu as pltpu

F32 = jnp.float32
BF16 = jnp.bfloat16

RMS_EPS = 1e-6
POOL_WINDOWS = (2, 4, 8, 16)
POOL_BUF = max(POOL_WINDOWS) - 1
DN_HEADS = 8
DN_DK = 128
DN_CONV = 4
DN_CHUNK = 64
SB_HEADS = 16
SB_DH = 64
FFN_CONV = 3

LANES = 128
MXU_N = 256
VMEM_LIMIT = 56 * 1024 * 1024
ROW_TILE = 512
SB_KEY_GROUP = 4
SB_SOFTPLUS_LINEAR = 30.0
LOG2E = 1.4426950408889634


def _params(n_axes, vmem=VMEM_LIMIT):
    return pltpu.CompilerParams(dimension_semantics=("arbitrary",) * n_axes, vmem_limit_bytes=vmem)


def _const_spec(shape):
    nd = len(shape)
    return pl.BlockSpec(shape, lambda *_: (0,) * nd, pipeline_mode=pl.Buffered(1))


def _rms_rows(x, g):
    ms = jnp.mean(x * x, axis=-1, keepdims=True)
    return x * lax.rsqrt(ms + RMS_EPS) * g


def _silu(x):
    return x * jax.nn.sigmoid(x)


def _softplus(x):
    return jnp.maximum(x, 0.0) + jnp.log1p(jnp.exp(-jnp.abs(x)))


def _dot(a, b):
    return jnp.dot(a, b, preferred_element_type=F32)


def _dot_nt(a, b, precision=None):
    return lax.dot_general(a, b, (((1,), (1,)), ((), ())), preferred_element_type=F32, precision=precision)


def _tiling(B, L):
    tl = min(L, ROW_TILE)
    nb = max(1, min(B, ROW_TILE // tl))
    assert L % tl == 0 and B % nb == 0
    return nb, tl


def _ffn_body(*refs, nb, tl, n_chunks, final):
    (x_ref, buf_ref, ng_ref, wv_ref, wg_ref, cw_ref, cb_ref, wd_ref) = refs[:8]
    k = 8
    if final:
        fg_ref = refs[k]
        k += 1
    o_ref, nbuf_ref = refs[k], refs[k + 1]
    k += 2
    if final:
        y_ref = refs[k]
        k += 1
    h_scr, acc_scr, cv_scr, carry_scr, hv_scr, hg_scr = refs[k:]
    t = pl.program_id(1)
    tm = nb * tl
    d = x_ref.shape[-1]
    pad = 8

    x = x_ref[...].reshape(tm, d)
    h_scr[...] = _rms_rows(x, ng_ref[...]).astype(BF16)
    acc_scr[...] = jnp.zeros_like(acc_scr)

    @pl.when(t == 0)
    def _():
        carry_scr[...] = buf_ref[...]

    def up(j, slot):
        h = h_scr[...]
        hv_scr[slot] = _dot(h, wv_ref[j])
        hg_scr[slot] = _dot(h, wg_ref[j])

    def down(j, slot):
        cw = cw_ref[j]
        cb = cb_ref[j]
        acts = []
        for b in range(nb):
            gb = hg_scr[slot, b * tl:(b + 1) * tl, :]
            cv_scr[b, pad - 2:pad, :] = carry_scr[b, j]
            cv_scr[b, pad:pad + tl, :] = gb
            conv = (cv_scr[b, pad - 2:pad - 2 + tl, :] * cw[0:1] + cv_scr[b, pad - 1:pad - 1 + tl, :] * cw[1:2]
                    + gb * cw[2:3])
            last = gb[tl - 2:tl]
            carry_scr[b, j] = last
            nbuf_ref[b, j] = last
            act = _silu(conv + cb) * hv_scr[slot, b * tl:(b + 1) * tl, :]
            acts.append(act.astype(BF16))
        act = acts[0] if nb == 1 else jnp.concatenate(acts, axis=0)
        acc_scr[...] += _dot(act, wd_ref[j])

    up(0, 0)

    def pair(i, c):
        j = 2 * i
        up(j + 1, 1)
        down(j, 0)
        up(j + 2, 0)
        down(j + 1, 1)
        return c

    n_pairs = (n_chunks - 1) // 2
    lax.fori_loop(0, n_pairs, pair, 0)
    if n_chunks % 2 == 0:
        up(n_chunks - 1, 1)
        down(n_chunks - 2, 0)
        down(n_chunks - 1, 1)
    else:
        down(n_chunks - 1, 0)
    out = x_ref[...].reshape(tm, d) + acc_scr[...]
    o_ref[...] = out.reshape(nb, tl, d)
    if final:
        y_ref[...] = _rms_rows(out, fg_ref[...]).reshape(nb, tl, d)


def _conv_ffn(x, buf, norm_g, w_up, conv_w, conv_b, w_down, final_g=None):
    B, L, D = x.shape
    F = w_down.shape[0]
    tf = MXU_N
    n_chunks = F // tf
    assert F % tf == 0
    nb, tl = _tiling(B, L)
    tm = nb * tl
    final = final_g is not None

    wv = w_up[:, :F].astype(BF16).reshape(D, n_chunks, tf).transpose(1, 0, 2)
    wg = w_up[:, F:].astype(BF16).reshape(D, n_chunks, tf).transpose(1, 0, 2)
    wd = w_down.astype(BF16).reshape(n_chunks, tf, D)
    cw = conv_w.reshape(FFN_CONV, n_chunks, tf).transpose(1, 0, 2)
    cb = conv_b.reshape(n_chunks, 1, tf)
    bufc = buf.astype(F32).reshape(B, FFN_CONV - 1, n_chunks, tf).transpose(0, 2, 1, 3)

    row_spec = pl.BlockSpec((nb, tl, D), lambda i, t: (i, t, 0))
    buf_spec = pl.BlockSpec((nb, n_chunks, FFN_CONV - 1, tf), lambda i, t: (i, 0, 0, 0))
    in_specs = [row_spec, buf_spec, _const_spec((1, D)), _const_spec(wv.shape), _const_spec(wg.shape),
                _const_spec(cw.shape), _const_spec(cb.shape), _const_spec(wd.shape)]
    args = [x, bufc, norm_g.reshape(1, D), wv, wg, cw, cb, wd]
    out_shape = [jax.ShapeDtypeStruct((B, L, D), F32),
                 jax.ShapeDtypeStruct((B, n_chunks, FFN_CONV - 1, tf), F32)]
    out_specs = [row_spec, buf_spec]
    if final:
        in_specs.append(_const_spec((1, D)))
        args.append(final_g.reshape(1, D))
        out_shape.append(jax.ShapeDtypeStruct((B, L, D), F32))
        out_specs.append(row_spec)
    scratch = [pltpu.VMEM((tm, D), BF16), pltpu.VMEM((tm, D), F32),
               pltpu.VMEM((nb, tl + 8, tf), F32), pltpu.VMEM((nb, n_chunks, FFN_CONV - 1, tf), F32),
               pltpu.VMEM((2, tm, tf), F32), pltpu.VMEM((2, tm, tf), F32)]
    outs = pl.pallas_call(
        functools.partial(_ffn_body, nb=nb, tl=tl, n_chunks=n_chunks, final=final),
        grid=(B // nb, L // tl), in_specs=in_specs, out_specs=out_specs, out_shape=out_shape,
        scratch_shapes=scratch, compiler_params=_params(2), name="conv_ffn")(*args)
    nbuf = outs[1].transpose(0, 2, 1, 3).reshape(B, FFN_CONV - 1, F)
    return outs[0], nbuf, (outs[2] if final else None)


def _pool_body(x_ref, buf_ref, ng_ref, w_ref, sc_ref, o_ref, nbuf_ref, ext_scr, *, nb, tl, start):
    t = pl.program_id(1)
    d = x_ref.shape[-1]
    hist = POOL_BUF + 1
    gc = d // len(POOL_WINDOWS)
    row = lax.broadcasted_iota(jnp.int32, (tl, 1), 0)
    n_avail = start + t * tl + row + 1
    for b in range(nb):
        @pl.when(t == 0)
        def _():
            ext_scr[b, 0:hist, :] = jnp.zeros((hist, d), F32)
            ext_scr[b, 1:hist, :] = buf_ref[b]

        @pl.when(t > 0)
        def _():
            ext_scr[b, 0:hist, :] = ext_scr[b, tl:tl + hist, :]

        x = x_ref[b]
        h = _rms_rows(x, ng_ref[...])
        ext_scr[b, hist:hist + tl, :] = h
        for g, win in enumerate(POOL_WINDOWS):
            cols = slice(g * gc, (g + 1) * gc)
            s = h[:, cols]
            for i in range(1, win):
                s = s + ext_scr[b, hist - i:hist - i + tl, cols]
            cnt = jnp.minimum(n_avail, win).astype(F32)
            dmean = s * (1.0 / cnt) - h[:, cols]
            y = _dot(dmean.astype(BF16), w_ref[g])
            o_ref[b, :, cols] = x[:, cols] + y * sc_ref[:, cols]
        nbuf_ref[b] = ext_scr[b, tl + 1:tl + hist, :]


def _pool_mixer(x, buf, norm_g, w, scale, start):
    B, L, D = x.shape
    nb, tl = _tiling(B, L)
    row_spec = pl.BlockSpec((nb, tl, D), lambda i, t: (i, t, 0))
    buf_spec = pl.BlockSpec((nb, POOL_BUF, D), lambda i, t: (i, 0, 0))
    wb = w.astype(BF16)
    return pl.pallas_call(
        functools.partial(_pool_body, nb=nb, tl=tl, start=start),
        grid=(B // nb, L // tl),
        in_specs=[row_spec, buf_spec, _const_spec((1, D)), _const_spec(wb.shape), _const_spec((1, D))],
        out_specs=[row_spec, buf_spec],
        out_shape=[jax.ShapeDtypeStruct((B, L, D), F32), jax.ShapeDtypeStruct((B, POOL_BUF, D), F32)],
        scratch_shapes=[pltpu.VMEM((nb, tl + POOL_BUF + 1, D), F32)],
        compiler_params=_params(2), name="pool_mixer")(x, buf.astype(F32), norm_g.reshape(1, D), wb,
                                                      scale.reshape(1, D))


def _gdn_in_body(x_ref, cbuf_ref, ng_ref, wqkv_ref, cw_ref, wz_ref, wab_ref, alog_ref, dtb_ref,
                 q_ref, k_ref, v_ref, z_ref, gb_ref, ncbuf_ref, h_scr, cv_scr, carry_scr, *, nb, tl):
    t = pl.program_id(1)
    tm = nb * tl
    d = x_ref.shape[-1]
    pad = 8
    tc = MXU_N
    n_sec = d // tc
    hw = DN_CONV - 1

    x = x_ref[...].reshape(tm, d)
    h_scr[...] = _rms_rows(x, ng_ref[...]).astype(BF16)
    h = h_scr[...]

    @pl.when(t == 0)
    def _():
        carry_scr[...] = cbuf_ref[...]

    outs = (q_ref, k_ref, v_ref)
    for j in range(3 * n_sec):
        sec, jc = divmod(j, n_sec)
        y = _dot(h, wqkv_ref[j])
        cw = cw_ref[j]
        for b in range(nb):
            yb = y[b * tl:(b + 1) * tl]
            cv_scr[b, pad - hw:pad, :] = carry_scr[b, j]
            cv_scr[b, pad:pad + tl, :] = yb
            conv = yb * cw[hw:hw + 1]
            for i in range(hw):
                conv = conv + cv_scr[b, pad - hw + i:pad - hw + i + tl, :] * cw[i:i + 1]
            last = yb[tl - hw:tl]
            carry_scr[b, j] = last
            ncbuf_ref[b, j] = last
            s = _silu(conv)
            for half in range(tc // DN_DK):
                sh = s[:, half * DN_DK:(half + 1) * DN_DK]
                if sec < 2:
                    sh = sh * lax.rsqrt(jnp.sum(sh * sh, axis=-1, keepdims=True) + RMS_EPS)
                if sec == 0:
                    sh = sh * (DN_DK ** -0.5)
                c0 = jc * tc + half * DN_DK
                outs[sec][b, :, c0:c0 + DN_DK] = sh
    for j in range(n_sec):
        z_ref[:, :, j * tc:(j + 1) * tc] = _dot(h, wz_ref[j]).reshape(nb, tl, tc)
    ab = _dot(h, wab_ref[...])
    g = -jnp.exp(alog_ref[...]) * _softplus(ab + dtb_ref[...])
    lane = lax.broadcasted_iota(jnp.int32, ab.shape, 1)
    gb_ref[...] = jnp.where(lane < DN_HEADS, g, jax.nn.sigmoid(ab)).reshape(nb, tl, LANES)


def _gdn_in(x, cbuf, norm_g, w_in, conv_w, a_log, dt_bias):
    B, L, D = x.shape
    nb, tl = _tiling(B, L)
    tm = nb * tl
    tc = MXU_N
    qkv_w = 3 * D
    n_qkv = qkv_w // tc
    n_sec = D // tc
    wqkv = w_in[:, :qkv_w].astype(BF16).reshape(D, n_qkv, tc).transpose(1, 0, 2)
    wz = w_in[:, qkv_w:qkv_w + D].astype(BF16).reshape(D, n_sec, tc).transpose(1, 0, 2)
    wab = jnp.zeros((D, LANES), BF16).at[:, :2 * DN_HEADS].set(w_in[:, qkv_w + D:].astype(BF16))
    cw = conv_w.reshape(DN_CONV, n_qkv, tc).transpose(1, 0, 2)
    alog = jnp.zeros((1, LANES), F32).at[0, :DN_HEADS].set(a_log)
    dtb = jnp.zeros((1, LANES), F32).at[0, :DN_HEADS].set(dt_bias)
    cbufc = cbuf.astype(F32).reshape(B, DN_CONV - 1, n_qkv, tc).transpose(0, 2, 1, 3)

    row_spec = pl.BlockSpec((nb, tl, D), lambda i, t: (i, t, 0))
    gb_spec = pl.BlockSpec((nb, tl, LANES), lambda i, t: (i, t, 0))
    cb_spec = pl.BlockSpec((nb, n_qkv, DN_CONV - 1, tc), lambda i, t: (i, 0, 0, 0))
    act = jax.ShapeDtypeStruct((B, L, D), F32)
    outs = pl.pallas_call(
        functools.partial(_gdn_in_body, nb=nb, tl=tl),
        grid=(B // nb, L // tl),
        in_specs=[row_spec, cb_spec, _const_spec((1, D)), _const_spec(wqkv.shape), _const_spec(cw.shape),
                  _const_spec(wz.shape), _const_spec(wab.shape), _const_spec((1, LANES)),
                  _const_spec((1, LANES))],
        out_specs=[row_spec, row_spec, row_spec, row_spec, gb_spec, cb_spec],
        out_shape=[act, act, act, act, jax.ShapeDtypeStruct((B, L, LANES), F32),
                   jax.ShapeDtypeStruct((B, n_qkv, DN_CONV - 1, tc), F32)],
        scratch_shapes=[pltpu.VMEM((tm, D), BF16), pltpu.VMEM((nb, tl + 8, tc), F32),
                        pltpu.VMEM((nb, n_qkv, DN_CONV - 1, tc), F32)],
        compiler_params=_params(2), name="gdn_in")(x, cbufc, norm_g.reshape(1, D), wqkv, cw, wz, wab, alog, dtb)
    q, k, v, z, gb, ncbuf = outs
    return q, k, v, z, gb, ncbuf.transpose(0, 2, 1, 3).reshape(B, DN_CONV - 1, qkv_w)


def _gdn_delta_body(q_ref, k_ref, v_ref, gb_ref, s0_ref, o_ref, sout_ref, s_scr, *, n_chunks):
    c_idx = pl.program_id(1)
    C = DN_CHUNK
    dk = DN_DK
    hp = lax.Precision.HIGHEST

    @pl.when(c_idx == 0)
    def _():
        s_scr[...] = s0_ref[0]

    ii = lax.broadcasted_iota(jnp.int32, (C, C), 0)
    jj = lax.broadcasted_iota(jnp.int32, (C, C), 1)
    lower_incl = ii >= jj
    lower_strict = ii > jj
    tri = lower_incl.astype(F32)
    eye = (ii == jj).astype(F32)
    zpad = jnp.zeros((LANES - C, LANES), F32)

    def chunk(ci, carry):
        r0 = pl.multiple_of(ci * C, C)
        rows = pl.ds(r0, C)
        gb = gb_ref[0, rows, :]
        gcs = jnp.dot(tri, gb, preferred_element_type=F32, precision=hp)
        gcs_t = jnp.concatenate([gcs, zpad], axis=0).T
        heads = range(DN_HEADS)
        tinv, qpow, attn, rhs, qg, kg_t, glast = [], [], [], [], [], [], []
        for h in heads:
            cols = slice(h * dk, (h + 1) * dk)
            q = q_ref[0, rows, cols]
            k = k_ref[0, rows, cols]
            v = v_ref[0, rows, cols]
            gcol = gcs[:, h:h + 1]
            grow = gcs_t[h:h + 1, 0:C]
            bcol = gb[:, DN_HEADS + h:DN_HEADS + h + 1]
            diff = jnp.where(lower_incl, gcol - grow, 0.0)
            dec_incl = jnp.where(lower_incl, jnp.exp(diff), 0.0)
            kb = k * bcol
            mq = _dot_nt(jnp.concatenate([kb, q], axis=0).astype(BF16), k.astype(BF16))
            p = jnp.where(lower_strict, -mq[:C] * dec_incl, 0.0)
            attn.append(mq[C:] * dec_incl)
            tinv.append(eye + p)
            qpow.append(p)
            egc = jnp.exp(gcol)
            rhs.append(jnp.concatenate([v * bcol, kb * egc], axis=1).astype(BF16))
            qg.append(q * egc)
            gl = grow[:, C - 1:C]
            glast.append(gl)
            k_t = jnp.concatenate([k, zpad], axis=0).T[:, 0:C]
            kg_t.append(k_t * jnp.exp(gl - grow))
        for h in heads:
            pb = qpow[h].astype(BF16)
            qpow[h] = _dot(pb, pb)
        for step in range(5):
            for h in heads:
                qb = qpow[h].astype(BF16)
                if step < 4:
                    tq2 = _dot(jnp.concatenate([tinv[h], qpow[h]], axis=0).astype(BF16), qb)
                    tinv[h] = tinv[h] + tq2[:C]
                    qpow[h] = tq2[C:]
                else:
                    tinv[h] = tinv[h] + _dot(tinv[h].astype(BF16), qb)
        uw = [_dot(tinv[h].astype(BF16), rhs[h]) for h in heads]
        s_old = [s_scr[h] for h in heads]
        ws_qs = [_dot(jnp.concatenate([uw[h][:, dk:], qg[h]], axis=0).astype(BF16), s_old[h].astype(BF16))
                 for h in heads]
        for h in heads:
            v_new = uw[h][:, :dk] - ws_qs[h][:C]
            x = _dot(jnp.concatenate([attn[h], kg_t[h]], axis=0).astype(BF16), v_new.astype(BF16))
            o_ref[0, rows, h * dk:(h + 1) * dk] = ws_qs[h][C:] + x[:C]
            s_scr[h] = s_old[h] * jnp.exp(glast[h]) + x[C:]
        return carry

    lax.fori_loop(0, n_chunks, chunk, 0)
    sout_ref[0] = s_scr[...]


def _gdn_delta(q, k, v, gb, s0):
    B, L, D = q.shape
    G = min(L, 1024)
    assert L % G == 0 and G % DN_CHUNK == 0
    row_spec = pl.BlockSpec((1, G, D), lambda b, c: (b, c, 0))
    gb_spec = pl.BlockSpec((1, G, LANES), lambda b, c: (b, c, 0))
    s_spec = pl.BlockSpec((1,) + s0.shape[1:], lambda b, c: (b, 0, 0, 0))
    return pl.pallas_call(
        functools.partial(_gdn_delta_body, n_chunks=G // DN_CHUNK),
        grid=(B, L // G),
        in_specs=[row_spec, row_spec, row_spec, gb_spec, s_spec],
        out_specs=[row_spec, s_spec],
        out_shape=[jax.ShapeDtypeStruct((B, L, D), F32), jax.ShapeDtypeStruct(s0.shape, F32)],
        scratch_shapes=[pltpu.VMEM(s0.shape[1:], F32)],
        compiler_params=_params(2), name="gdn_delta")(q, k, v, gb, s0.astype(F32))


def _gdn_out_body(x_ref, o_ref, z_ref, nw_ref, w_ref, out_ref, y_scr, *, nb, tl):
    tm = nb * tl
    d = x_ref.shape[-1]
    tc = MXU_N
    o = o_ref[...].reshape(tm, d)
    z = z_ref[...].reshape(tm, d)
    for h in range(d // DN_DK):
        cols = slice(h * DN_DK, (h + 1) * DN_DK)
        oh = o[:, cols]
        y_scr[:, cols] = (_rms_rows(oh, nw_ref[...]) * _silu(z[:, cols])).astype(BF16)
    y = y_scr[...]
    x = x_ref[...].reshape(tm, d)
    for j in range(d // tc):
        cols = slice(j * tc, (j + 1) * tc)
        out_ref[:, :, cols] = (x[:, cols] + _dot(y, w_ref[j])).reshape(nb, tl, tc)


def _gdn_out(x, o, z, norm_w, w_out):
    B, L, D = x.shape
    nb, tl = _tiling(B, L)
    tc = MXU_N
    w = w_out.astype(BF16).reshape(D, D // tc, tc).transpose(1, 0, 2)
    row_spec = pl.BlockSpec((nb, tl, D), lambda i, t: (i, t, 0))
    return pl.pallas_call(
        functools.partial(_gdn_out_body, nb=nb, tl=tl),
        grid=(B // nb, L // tl),
        in_specs=[row_spec, row_spec, row_spec, _const_spec((1, DN_DK)), _const_spec(w.shape)],
        out_specs=row_spec, out_shape=jax.ShapeDtypeStruct((B, L, D), F32),
        scratch_shapes=[pltpu.VMEM((nb * tl, D), BF16)],
        compiler_params=_params(2), name="gdn_out")(x, o, z, norm_w.reshape(1, DN_DK), w)


def _gated_delta_mixer(x, cbuf, s0, norm_g, w_in, conv_w, a_log, dt_bias, norm_w, w_out):
    q, k, v, z, gb, ncbuf = _gdn_in(x, cbuf, norm_g, w_in, conv_w, a_log, dt_bias)
    o, s_new = _gdn_delta(q, k, v, gb, s0)
    return _gdn_out(x, o, z, norm_w, w_out), ncbuf, s_new.astype(s0.dtype)


def _sb_qkv_body(x_ref, ng_ref, w_ref, q_ref, k_ref, v_ref, *, nb, tl):
    tm = nb * tl
    d = x_ref.shape[-1]
    tc = MXU_N
    n_sec = d // tc
    h = _rms_rows(x_ref[...].reshape(tm, d), ng_ref[...]).astype(BF16)
    outs = (q_ref, k_ref, v_ref)
    for j in range(3 * n_sec):
        sec, jc = divmod(j, n_sec)
        y = _dot(h, w_ref[j])
        if sec == 0:
            y = y * (SB_DH ** -0.5 * LOG2E)
        outs[sec][:, :, jc * tc:(jc + 1) * tc] = y.reshape(nb, tl, tc)


def _sb_qkv(x, norm_g, w_qkv):
    B, L, D = x.shape
    nb, tl = _tiling(B, L)
    tc = MXU_N
    w = w_qkv.astype(BF16).reshape(D, 3 * D // tc, tc).transpose(1, 0, 2)
    row_spec = pl.BlockSpec((nb, tl, D), lambda i, t: (i, t, 0))
    act = jax.ShapeDtypeStruct((B, L, D), F32)
    return pl.pallas_call(
        functools.partial(_sb_qkv_body, nb=nb, tl=tl),
        grid=(B // nb, L // tl),
        in_specs=[row_spec, _const_spec((1, D)), _const_spec(w.shape)],
        out_specs=[row_spec, row_spec, row_spec], out_shape=[act, act, act],
        compiler_params=_params(2), name="sb_qkv")(x, norm_g.reshape(1, D), w)


def _sb_attn_body(q_ref, k_ref, v_ref, o_ref, kb_scr, vb_scr, u_scr, qs_scr, acc_scr, c_scr, *, tq, tk, q_start):
    qi = pl.program_id(2)

    @pl.when(qi == 0)
    def _():
        kb_scr[...] = k_ref[0].astype(BF16)
        vb_scr[...] = v_ref[0].astype(BF16)
        ui = lax.broadcasted_iota(jnp.int32, (tk, tk), 0)
        uj = lax.broadcasted_iota(jnp.int32, (tk, tk), 1)
        u_scr[...] = (ui >= uj).astype(BF16)

    p0 = q_start + qi * tq
    n_full = p0 // tk
    q2 = q_ref[0]
    lo = lax.broadcasted_iota(jnp.int32, (tq, LANES), 1) < SB_DH
    qs_scr[0:tq, :] = jnp.where(lo, q2, 0.0).astype(BF16)
    qs_scr[tq:2 * tq, :] = jnp.where(lo, 0.0, q2).astype(BF16)
    acc_scr[...] = jnp.zeros_like(acc_scr)
    c_scr[...] = jnp.zeros_like(c_scr)

    def block(j_lo, nblk, masked):
        k0 = pl.multiple_of(j_lo * tk, tk)
        w = nblk * tk
        z = _dot_nt(qs_scr[...], kb_scr[pl.ds(k0, w), :])
        sp = jnp.where(z > SB_SOFTPLUS_LINEAR, z, jnp.log2(1.0 + jnp.exp2(z)))
        if masked:
            row = lax.broadcasted_iota(jnp.int32, (2 * tq, w), 0)
            qpos = p0 + (row & (tq - 1))
            kpos = k0 + lax.broadcasted_iota(jnp.int32, (2 * tq, w), 1)
            valid = kpos < qpos
            sp = jnp.where(valid, sp, 0.0)
        spb = sp.astype(BF16)
        c = c_scr[...]
        a_parts = [None] * nblk
        for i in reversed(range(nblk)):
            cols = slice(i * tk, (i + 1) * tk)
            incl = _dot(spb[:, cols], u_scr[...])
            e = z[:, cols] - incl - jnp.concatenate([c] * (tk // LANES), axis=1)
            if masked:
                e = jnp.where(valid[:, cols], e, -1e30)
            a_parts[i] = jnp.exp2(e).astype(BF16)
            c = c + jnp.broadcast_to(incl[:, 0:1], (2 * tq, LANES))
        a = a_parts[0] if nblk == 1 else jnp.concatenate(a_parts, axis=1)
        acc_scr[...] += _dot(a, vb_scr[pl.ds(k0, w), :])
        c_scr[...] = c

    block(n_full, 1, True)
    left = n_full
    size = SB_KEY_GROUP
    while size >= 1:
        trips = left // size

        def body(i, carry, left=left, size=size):
            block(left - (i + 1) * size, size, False)
            return carry

        lax.fori_loop(0, trips, body, 0)
        left = left - trips * size
        size //= 2
    o_ref[0] = jnp.where(lo, acc_scr[0:tq, :], acc_scr[tq:2 * tq, :])


def _sb_attend(q, k_all, v_all, q_start):
    B, Lq, D = q.shape
    Lk = k_all.shape[1]
    tk = MXU_N
    tq = min(Lq, tk)
    assert Lq % tq == 0 and Lk % tk == 0 and q_start % tk == 0 and q_start + Lq <= Lk and tq & (tq - 1) == 0
    q_spec = pl.BlockSpec((1, tq, LANES), lambda b, hp, i: (b, i, hp))
    kv_spec = pl.BlockSpec((1, Lk, LANES), lambda b, hp, i: (b, 0, hp))
    return pl.pallas_call(
        functools.partial(_sb_attn_body, tq=tq, tk=tk, q_start=q_start),
        grid=(B, D // LANES, Lq // tq),
        in_specs=[q_spec, kv_spec, kv_spec], out_specs=q_spec,
        out_shape=jax.ShapeDtypeStruct((B, Lq, D), F32),
        scratch_shapes=[pltpu.VMEM((Lk, LANES), BF16), pltpu.VMEM((Lk, LANES), BF16), pltpu.VMEM((tk, tk), BF16),
                        pltpu.VMEM((2 * tq, LANES), BF16), pltpu.VMEM((2 * tq, LANES), F32),
                        pltpu.VMEM((2 * tq, LANES), F32)],
        compiler_params=_params(3), name="sb_attn")(q, k_all, v_all)


def _proj_res_body(x_ref, o_ref, w_ref, out_ref, *, nb, tl):
    tm = nb * tl
    d = x_ref.shape[-1]
    tc = MXU_N
    y = o_ref[...].reshape(tm, d).astype(BF16)
    x = x_ref[...].reshape(tm, d)
    for j in range(d // tc):
        cols = slice(j * tc, (j + 1) * tc)
        out_ref[:, :, cols] = (x[:, cols] + _dot(y, w_ref[j])).reshape(nb, tl, tc)


def _proj_res(x, o, w_out):
    B, L, D = x.shape
    nb, tl = _tiling(B, L)
    tc = MXU_N
    w = w_out.astype(BF16).reshape(D, D // tc, tc).transpose(1, 0, 2)
    row_spec = pl.BlockSpec((nb, tl, D), lambda i, t: (i, t, 0))
    return pl.pallas_call(
        functools.partial(_proj_res_body, nb=nb, tl=tl),
        grid=(B // nb, L // tl),
        in_specs=[row_spec, row_spec, _const_spec(w.shape)],
        out_specs=row_spec, out_shape=jax.ShapeDtypeStruct((B, L, D), F32),
        compiler_params=_params(2), name="proj_res")(x, o, w)


def _sb_mixer(x, k_past, v_past, norm_g, w_qkv, w_out, start):
    B, L, D = x.shape
    q, k, v = _sb_qkv(x, norm_g, w_qkv)
    if k_past is None:
        k_all, v_all = k, v
    else:
        past = k_past.shape[1]
        lk = -(-(past + L) // MXU_N) * MXU_N
        tail = jnp.zeros((B, lk - past - L, D), F32)
        k_all = jnp.concatenate([k_past.reshape(B, past, D).astype(F32), k, tail], axis=1)
        v_all = jnp.concatenate([v_past.reshape(B, past, D).astype(F32), v, tail], axis=1)
    o = _sb_attend(q, k_all, v_all, start)
    return (_proj_res(x, o, w_out), k.reshape(B, L, SB_HEADS, SB_DH), v.reshape(B, L, SB_HEADS, SB_DH))


def _trunk(x, pool_bufs, dn_conv_bufs, dn_states, sb_k_past, sb_v_past, ffn_bufs, start, p):
    depth = p['mix_norm'].shape[0]
    n_pool, n_dnc, n_dn, n_k, n_v, n_ffn = [], [], [], [], [], []
    y = None
    for i in range(depth):
        kind, j = i % 3, i // 3
        if kind == 0:
            x, buf = _pool_mixer(x, pool_bufs[j], p['mix_norm'][i], p['pool_w'][j], p['pool_scale'][j], start)
            n_pool.append(buf)
        elif kind == 1:
            x, cbuf, s = _gated_delta_mixer(x, dn_conv_bufs[j], dn_states[j], p['mix_norm'][i], p['dn_w_in'][j],
                                            p['dn_conv_w'][j], p['dn_a_log'][j], p['dn_dt_bias'][j],
                                            p['dn_norm'][j], p['dn_w_out'][j])
            n_dnc.append(cbuf)
            n_dn.append(s)
        else:
            kp = None if sb_k_past is None else sb_k_past[j]
            vp = None if sb_v_past is None else sb_v_past[j]
            x, kn, vn = _sb_mixer(x, kp, vp, p['mix_norm'][i], p['sb_w_qkv'][j], p['sb_w_out'][j], start)
            n_k.append(kn)
            n_v.append(vn)
        fg = p['final_norm'] if i == depth - 1 else None
        x, fbuf, y = _conv_ffn(x, ffn_bufs[i], p['ffn_norm'][i], p['ffn_w_up'][i], p['ffn_conv_w'][i],
                               p['ffn_conv_b'][i], p['ffn_w_down'][i], fg)
        n_ffn.append(fbuf)
    return (y, jnp.stack(n_pool), jnp.stack(n_dnc), jnp.stack(n_dn), jnp.stack(n_k), jnp.stack(n_v),
            jnp.stack(n_ffn))


def kernel(x_prompt, x_sample, state_pool, state_dn_conv, state_dn, cache_sb_k, cache_sb_v, state_ffn_conv,
           mix_norm, ffn_norm, final_norm, pool_w, pool_scale, dn_w_in, dn_conv_w, dn_a_log, dn_dt_bias,
           dn_norm, dn_w_out, sb_w_qkv, sb_w_out, ffn_w_up, ffn_conv_w, ffn_conv_b, ffn_w_down):
    p = dict(mix_norm=mix_norm, ffn_norm=ffn_norm, final_norm=final_norm, pool_w=pool_w, pool_scale=pool_scale,
             dn_w_in=dn_w_in, dn_conv_w=dn_conv_w, dn_a_log=dn_a_log, dn_dt_bias=dn_dt_bias, dn_norm=dn_norm,
             dn_w_out=dn_w_out, sb_w_qkv=sb_w_qkv, sb_w_out=sb_w_out, ffn_w_up=ffn_w_up, ffn_conv_w=ffn_conv_w,
             ffn_conv_b=ffn_conv_b, ffn_w_down=ffn_w_down)
    bp = x_prompt.shape[0]
    dtp = x_prompt.dtype
    zero_pool = jnp.zeros((state_pool.shape[0], bp) + state_pool.shape[2:], dtp)
    zero_dnc = jnp.zeros((state_dn_conv.shape[0], bp) + state_dn_conv.shape[2:], dtp)
    zero_dn = jnp.zeros((state_dn.shape[0], bp) + state_dn.shape[2:], state_dn.dtype)
    zero_ffn = jnp.zeros((state_ffn_conv.shape[0], bp) + state_ffn_conv.shape[2:], dtp)
    y_prompt, pool_p, dnc_p, dn_p, k_p, v_p, ffn_p = _trunk(
        x_prompt, zero_pool, zero_dnc, zero_dn, None, None, zero_ffn, 0, p)
    past_len = cache_sb_k.shape[2]
    y_sample, pool_s, dnc_s, dn_s, k_s, v_s, ffn_s = _trunk(
        x_sample, state_pool, state_dn_conv, state_dn, cache_sb_k, cache_sb_v, state_ffn_conv, past_len, p)
    return (y_prompt, y_sample, pool_p, pool_s, dnc_p, dnc_s, dn_p, dn_s, k_p, k_s, v_p, v_s, ffn_p, ffn_s)
```
